```python
import jax
import jax.numpy as jnp
from jax import lax
import numpy as np

D_MODEL = 1024
BATCH = 8
SEQ = 2048
DEPTH = 1
DEC_BATCH = 128
DEC_SEQ = 8
PAST_LEN = 2048
PAGE_SIZE = 128

HEAD_DIM = 64
N_HEADS = D_MODEL // HEAD_DIM
HA = N_HEADS // 2
HB = N_HEADS - HA
DK = HEAD_DIM
DV = HEAD_DIM
WA = HA * DV
WB = HB * HEAD_DIM
MIX_WIDTH = WA + WB
CHUNK_A = 64
Q_BLOCK = 128
SB_BIAS_INIT = -6.0
PEER_HEADS = 8
PEER_NKEYS = 128
PEER_N = PEER_NKEYS * PEER_NKEYS
PEER_DQ = 256
PEER_TOPK = 16
PEER_BLOCK = 256
ALPHA = (2.0 * DEPTH) ** 0.25
BETA_INIT = (8.0 * DEPTH) ** -0.25
LN_EPS = 1e-5
RMS_EPS = 1e-6
F32 = jnp.float32
SPLITS = (HA * DK, 2 * HA * DK, 2 * HA * DK + WA, 2 * HA * DK + 2 * WA, 2 * HA * DK + 2 * WA + WB, 2 * HA * DK + 2 * WA + 2 * WB)
IN_WIDTH = 2 * HA * DK + 2 * WA + 3 * WB

kernel_name = 'hymba_hgrn2_stickbreaking_peer_step'


def _layernorm(x, g, b):
    xf = x.astype(F32)
    mu = jnp.mean(xf, axis=-1, keepdims=True)
    var = jnp.mean(jnp.square(xf - mu), axis=-1, keepdims=True)
    return ((xf - mu) * lax.rsqrt(var + LN_EPS) * g.astype(F32) + b.astype(F32)).astype(x.dtype)


def _head_rmsnorm(o, g):
    of = o.astype(F32)
    of = of * lax.rsqrt(jnp.mean(of * of, axis=-1, keepdims=True) + RMS_EPS)
    return of.reshape(*o.shape[:-2], -1) * g.astype(F32)


def _hgrn2_chunked(q, logf, k, v, s0):
    B, T = q.shape[0], q.shape[1]
    C = CHUNK_A if T % CHUNK_A == 0 else T
    n = T // C

    def to_chunks(a):
        return jnp.moveaxis(a.reshape(B, n, C, *a.shape[2:]), 1, 0).astype(F32)

    causal = jnp.tril(jnp.ones((C, C), dtype=bool))

    def step(S, inp):
        qc, lfc, kc, vc = inp
        b = jnp.cumsum(lfc, axis=1)
        o_inter = jnp.einsum('bchk,bhkv->bchv', qc * jnp.exp(b), S)
        diff = b[:, :, None] - b[:, None, :]
        decay = jnp.exp(jnp.where(causal[None, :, :, None, None], diff, -jnp.inf))
        scores = jnp.einsum('bthk,btshk,bshk->bhts', qc, decay, kc)
        o_intra = jnp.einsum('bhts,bshv->bthv', scores, vc)
        b_last = b[:, -1]
        S_new = jnp.exp(b_last)[..., None] * S + jnp.einsum('bshk,bshv->bhkv', kc * jnp.exp(b_last[:, None] - b), vc)
        return S_new, o_inter + o_intra

    S_fin, o = lax.scan(step, s0.astype(F32), (to_chunks(q), to_chunks(logf), to_chunks(k), to_chunks(v)))
    o = jnp.moveaxis(o, 0, 1).reshape(B, T, v.shape[2], v.shape[3])
    return o, S_fin


def _sb_block(qb, k, v, q_pos, k_pos, bias):
    z = jnp.einsum('bhqd,bhsd->bhqs', qb.astype(F32), k) * (HEAD_DIM ** -0.5) + bias.astype(F32)[None, :, None, None]
    mask = k_pos[None, :] < q_pos[:, None]
    log_surv = jnp.where(mask, jax.nn.log_sigmoid(-z), 0.0)
    tail = lax.cumsum(log_surv, axis=3, reverse=True) - log_surv
    a = jnp.where(mask, jnp.exp(jax.nn.log_sigmoid(z) + tail), 0.0)
    return jnp.einsum('bhqs,bhsd->bhqd', a, v)


def _stick_breaking(q, k, v, q_pos, k_pos, bias):
    B, T, H, d = q.shape
    qbl = Q_BLOCK if T % Q_BLOCK == 0 else T
    nq = T // qbl
    kt = jnp.swapaxes(k, 1, 2).astype(F32)
    vt = jnp.swapaxes(v, 1, 2).astype(F32)
    q_blocks = jnp.moveaxis(jnp.swapaxes(q, 1, 2).reshape(B, H, nq, qbl, d), 2, 0)
    pos_blocks = q_pos.reshape(nq, qbl)
    o = lax.map(lambda a: _sb_block(a[0], kt, vt, a[1], k_pos, bias), (q_blocks, pos_blocks))
    return jnp.swapaxes(jnp.moveaxis(o, 0, 2).reshape(B, H, T, d), 1, 2)


def _peer(h, wq, k1, k2, u, vtab):
    shp = h.shape
    x = h.reshape(-1, shp[-1])
    T = x.shape[0]
    n = -(-T // PEER_BLOCK)
    xp = jnp.pad(x, ((0, n * PEER_BLOCK - T), (0, 0)))

    def one(xb):
        q = (xb @ wq).reshape(PEER_BLOCK, PEER_HEADS, 2, PEER_DQ // 2).astype(F32)
        s1 = jnp.einsum('thd,nd->thn', q[:, :, 0], k1.astype(F32))
        s2 = jnp.einsum('thd,nd->thn', q[:, :, 1], k2.astype(F32))
        v1, i1 = lax.top_k(s1, PEER_TOPK)
        v2, i2 = lax.top_k(s2, PEER_TOPK)
        cand = (v1[..., :, None] + v2[..., None, :]).reshape(PEER_BLOCK, PEER_HEADS, PEER_TOPK * PEER_TOPK)
        cidx = (i1[..., :, None] * PEER_NKEYS + i2[..., None, :]).reshape(PEER_BLOCK, PEER_HEADS, PEER_TOPK * PEER_TOPK)
        top_s, pos = lax.top_k(cand, PEER_TOPK)
        eidx = jnp.take_along_axis(cidx, pos, axis=-1)
        g = jax.nn.softmax(top_s, axis=-1)
        act = jax.nn.gelu(jnp.einsum('td,thkd->thk', xb, u[eidx]).astype(F32))
        return jnp.einsum('thk,thkd->td', (g * act).astype(xb.dtype), vtab[eidx])

    y = lax.map(one, xp.reshape(n, PEER_BLOCK, shp[-1]))
    return y.reshape(n * PEER_BLOCK, shp[-1])[:T].reshape(shp)


def _layer(x, c, s0, k_past, v_past, w_ada, b_ada, w_in, lb, norm_a, norm_b, sb_bias, w_out,
           ln1_g, ln1_b, ln2_g, ln2_b, peer_wq, peer_k1, peer_k2, peer_u, peer_v):
    B, T = x.shape[0], x.shape[1]
    mod = (c @ w_ada + b_ada)[:, None, :]
    sh1, sc1, g1, sh2, sc2, g2 = jnp.split(mod, 6, axis=-1)
    h = x * (1.0 + sc1) + sh1
    proj = h @ w_in
    qa, fa, ia, ga, qb, kb, vb = jnp.split(proj, list(SPLITS), axis=-1)
    z = fa.astype(F32)
    logf = jnp.log(lb + (1.0 - lb) * jax.nn.sigmoid(z))
    kin = (1.0 - lb) * jax.nn.sigmoid(-z)
    o_a, S = _hgrn2_chunked(jax.nn.silu(qa).reshape(B, T, HA, DK), logf.reshape(B, T, HA, DK),
                            kin.reshape(B, T, HA, DK), ia.reshape(B, T, HA, DV), s0)
    o_a = _head_rmsnorm(o_a, norm_a) * jax.nn.silu(ga.astype(F32))
    kb4 = kb.reshape(B, T, HB, HEAD_DIM)
    vb4 = vb.reshape(B, T, HB, HEAD_DIM)
    if k_past is None:
        past_len = 0
        k_all, v_all = kb4, vb4
    else:
        past_len = k_past.shape[1]
        k_all = jnp.concatenate([k_past.astype(kb4.dtype), kb4], axis=1)
        v_all = jnp.concatenate([v_past.astype(vb4.dtype), vb4], axis=1)
    k_pos = jnp.arange(past_len + T)
    q_pos = past_len + jnp.arange(T)
    o_b = _stick_breaking(qb.reshape(B, T, HB, HEAD_DIM), k_all, v_all, q_pos, k_pos, sb_bias)
    o_b = _head_rmsnorm(o_b, norm_b)
    mix = jnp.concatenate([o_a, o_b], axis=-1).astype(x.dtype) @ w_out
    x = _layernorm(ALPHA * x + (1.0 + g1) * mix, ln1_g, ln1_b)
    h2 = x * (1.0 + sc2) + sh2
    ff = _peer(h2, peer_wq, peer_k1, peer_k2, peer_u, peer_v)
    x = _layernorm(ALPHA * x + (1.0 + g2) * ff, ln2_g, ln2_b)
    return x, S, kb4, vb4


def _normal(k, shape, scale):
    return jax.random.normal(k, shape, F32) * scale


def setup_inputs(seed: int = 0) -> dict:
    key = jax.random.key(seed)
    ks = jax.random.split(key, 25)
    n_pages = PAST_LEN // PAGE_SIZE
    n_used = DEC_BATCH * n_pages
    n_phys = n_used + n_used // 4
    perm = jax.random.permutation(ks[0], n_phys)
    page_table = perm[:n_used].reshape(DEC_BATCH, n_pages).astype(jnp.int32)
    col_scale = jnp.concatenate([jnp.ones((2 * HA * DK,), F32), jnp.full((WA,), BETA_INIT, F32),
                                 jnp.ones((WA + 2 * WB,), F32), jnp.full((WB,), BETA_INIT, F32)])
    return {
        'x_prompt': _normal(ks[1], (BATCH, SEQ, D_MODEL), 1.0),
        'x_sample': _normal(ks[2], (DEC_BATCH, DEC_SEQ, D_MODEL), 1.0),
        'cache_k': _normal(ks[3], (DEPTH, n_phys, PAGE_SIZE, HB, HEAD_DIM), 1.0),
        'cache_v': _normal(ks[4], (DEPTH, n_phys, PAGE_SIZE, HB, HEAD_DIM), 1.0),
        'state_hgrn': _normal(ks[5], (DEPTH, DEC_BATCH, HA, DK, DV), 0.5),
        'page_table': page_table,
        'c_prompt': _normal(ks[6], (BATCH, D_MODEL), 1.0),
        'c_sample': _normal(ks[7], (DEC_BATCH, D_MODEL), 1.0),
        'w_ada': _normal(ks[8], (DEPTH, D_MODEL, 6 * D_MODEL), 0.1 * D_MODEL ** -0.5),
        'b_ada': _normal(ks[9], (DEPTH, 6 * D_MODEL), 0.02),
        'w_in': _normal(ks[10], (DEPTH, D_MODEL, IN_WIDTH), D_MODEL ** -0.5) * col_scale,
        'lb_raw': _normal(ks[11], (DEPTH + 1, HA * DK), 0.5),
        'norm_a': 1.0 + _normal(ks[12], (DEPTH, WA), 0.01),
        'norm_b': 1.0 + _normal(ks[13], (DEPTH, WB), 0.01),
        'sb_bias': SB_BIAS_INIT + _normal(ks[24], (DEPTH, HB), 0.1),
        'w_out': _normal(ks[14], (DEPTH, MIX_WIDTH, D_MODEL), BETA_INIT * MIX_WIDTH ** -0.5),
        'ln1_g': 1.0 + _normal(ks[15], (DEPTH, D_MODEL), 0.01),
        'ln1_b': _normal(ks[16], (DEPTH, D_MODEL), 0.01),
        'ln2_g': 1.0 + _normal(ks[17], (DEPTH, D_MODEL), 0.01),
        'ln2_b': _normal(ks[18], (DEPTH, D_MODEL), 0.01),
        'peer_wq': _normal(ks[19], (DEPTH, D_MODEL, PEER_HEADS * PEER_DQ), D_MODEL ** -0.5),
        'peer_k1': _normal(ks[20], (DEPTH, PEER_NKEYS, PEER_DQ // 2), (PEER_DQ // 2) ** -0.5),
        'peer_k2': _normal(ks[21], (DEPTH, PEER_NKEYS, PEER_DQ // 2), (PEER_DQ // 2) ** -0.5),
        'peer_u': _normal(ks[22], (DEPTH, PEER_N, D_MODEL), D_MODEL ** -0.5),
        'peer_v': _normal(ks[23], (DEPTH, PEER_N, D_MODEL), BETA_INIT * 0.5),
    }


def reference(x_prompt, x_sample, cache_k, cache_v, state_hgrn, page_table, c_prompt, c_sample,
              w_ada, b_ada, w_in, lb_raw, norm_a, norm_b, sb_bias, w_out, ln1_g, ln1_b, ln2_g, ln2_b,
              peer_wq, peer_k1, peer_k2, peer_u, peer_v):
    lb_all = jnp.cumsum(jax.nn.softmax(lb_raw.astype(F32), axis=0), axis=0)
    db, n_pages = page_table.shape
    page, hb, dh = cache_k.shape[2], cache_k.shape[3], cache_k.shape[4]
    past_len = n_pages * page
    yp, ys = x_prompt, x_sample
    kp_l, vp_l, sp_l, ks_l, vs_l, ss_l = [], [], [], [], [], []
    for l in range(DEPTH):
        w = (w_ada[l], b_ada[l], w_in[l], lb_all[l], norm_a[l], norm_b[l], sb_bias[l], w_out[l],
             ln1_g[l], ln1_b[l], ln2_g[l], ln2_b[l], peer_wq[l], peer_k1[l], peer_k2[l], peer_u[l], peer_v[l])
        s0 = jnp.zeros((x_prompt.shape[0], HA, DK, DV), F32)
        yp, sp, kp, vp = _layer(yp, c_prompt, s0, None, None, *w)
        k_past = cache_k[l][page_table].reshape(db, past_len, hb, dh)
        v_past = cache_v[l][page_table].reshape(db, past_len, hb, dh)
        ys, ss, ksn, vsn = _layer(ys, c_sample, state_hgrn[l], k_past, v_past, *w)
        kp_l.append(kp); vp_l.append(vp); sp_l.append(sp)
        ks_l.append(ksn); vs_l.append(vsn); ss_l.append(ss)
    k_prompt = jnp.stack(kp_l)
    v_prompt = jnp.stack(vp_l)
    s_prompt = jnp.stack(sp_l)
    k_sample = jnp.stack(ks_l)
    v_sample = jnp.stack(vs_l)
    s_sample = jnp.stack(ss_l)
    return (yp, ys, k_prompt, v_prompt, s_prompt, k_sample, v_sample, s_sample)
```

```python
import functools
import math

import numpy as np
import jax
import jax.numpy as jnp
from jax import lax
from jax.experimental import pallas as pl
from jax.experimental.pallas import tpu as pltpu

F32 = jnp.float32
BF16 = jnp.bfloat16

HEAD_DIM = 64
LANES = 128
LN_EPS = 1e-5
RMS_EPS = 1e-6
PEER_TOPK = 16
VMEM_LIMIT = 56 * 1024 * 1024

_NT = (((1,), (1,)), ((), ()))
_TN = (((0,), (0,)), ((), ()))


def _dot(a, b):
    return jnp.dot(a, b, preferred_element_type=F32)


def _dot_nt(a, b):
    return lax.dot_general(a, b, _NT, preferred_element_type=F32)


def _split3(x):
    hi = x.astype(BF16)
    r1 = x - hi.astype(F32)
    mid = r1.astype(BF16)
    lo = (r1 - mid.astype(F32)).astype(BF16)
    return hi, mid, lo


def _params(*sem):
    return pltpu.CompilerParams(dimension_semantics=sem, vmem_limit_bytes=VMEM_LIMIT)


def _mod_kernel(c_ref, w_ref, b_ref, o_ref):
    o_ref[...] = _dot(c_ref[...].astype(BF16), w_ref[...].astype(BF16)) + b_ref[...]


def _mod(c, w, b):
    n, d = c.shape
    width = w.shape[1]
    tn = 1024
    return pl.pallas_call(
        _mod_kernel,
        out_shape=jax.ShapeDtypeStruct((n, width), F32),
        grid=(width // tn,),
        in_specs=[pl.BlockSpec((n, d), lambda j: (0, 0)),
                  pl.BlockSpec((d, tn), lambda j: (0, j)),
                  pl.BlockSpec((1, tn), lambda j: (0, j))],
        out_specs=pl.BlockSpec((n, tn), lambda j: (0, j)),
        compiler_params=_params("parallel"),
        name="mod",
    )(c, w, b.reshape(1, width))


class _ModSpec:
    def __init__(self, arr, per_token, tm, tiles_per_seq, d):
        self.arr, self.per_token, self.tm, self.tps, self.d = arr, per_token, tm, tiles_per_seq, d

    def spec(self, part, ngrid):
        d, tps = self.d, self.tps
        if self.per_token:
            if ngrid == 1:
                return pl.BlockSpec((self.tm, d), lambda i: (i, part))
            return pl.BlockSpec((self.tm, d), lambda i, j: (i, part))
        if ngrid == 1:
            return pl.BlockSpec((None, 1, d), lambda i: (i // tps, 0, part))
        return pl.BlockSpec((None, 1, d), lambda i, j: (i // tps, 0, part))


def _inproj_kernel(x_ref, sh_ref, sc_ref, w_ref, o_ref, h_ref):
    @pl.when(pl.program_id(1) == 0)
    def _():
        h_ref[...] = (x_ref[...] * (1.0 + sc_ref[...]) + sh_ref[...]).astype(BF16)

    o_ref[...] = _dot(h_ref[...], w_ref[...])


def _inproj(x, ms, w_bf16, tn):
    n, d = x.shape
    tm = ms.tm
    nparts = w_bf16.shape[1] // tn
    return pl.pallas_call(
        _inproj_kernel,
        out_shape=jax.ShapeDtypeStruct((nparts, n, tn), F32),
        grid=(n // tm, nparts),
        in_specs=[pl.BlockSpec((tm, d), lambda i, j: (i, 0)),
                  ms.spec(0, 2), ms.spec(1, 2),
                  pl.BlockSpec((d, tn), lambda i, j: (0, j))],
        out_specs=pl.BlockSpec((None, tm, tn), lambda i, j: (j, i, 0)),
        scratch_shapes=[pltpu.VMEM((tm, d), BF16)],
        compiler_params=_params("parallel", "arbitrary"),
        name="inproj",
    )(x, ms.arr, ms.arr, w_bf16)


def _hgrn_consts(c, lseq):
    nlev = int(math.log2(lseq))
    r = np.arange(c)
    seg, pos = r // lseq, r % lseq
    j = np.arange(c)
    same = seg[:, None] == seg[None, :]
    mats = [same & (j[None, :] <= r[:, None]),
            same & (j[None, :] > r[:, None])]
    rq = np.zeros((nlev, c, 1), np.float32)
    mb = np.zeros((nlev + 1, c, c), np.float32)
    for lev in range(nlev):
        half, blk = 1 << lev, 2 << lev
        is_q = (pos % blk) >= half
        ref = (r // blk) * blk + half - 1
        m = np.where(is_q[:, None],
                     (j[None, :] > ref[:, None]) & (j[None, :] <= r[:, None]),
                     (j[None, :] > r[:, None]) & (j[None, :] <= ref[:, None]))
        mats.append(m)
        rq[lev, :, 0] = is_q
        mb[lev] = (r[:, None] // blk) == (r[None, :] // blk)
    mb[nlev] = np.eye(c)
    a = np.concatenate(mats, axis=0).astype(np.float32)
    rq = np.broadcast_to(rq, (nlev, c, LANES)).copy()
    mb2 = np.concatenate([mb, mb], axis=1)
    return jnp.asarray(a, BF16), jnp.asarray(rq), jnp.asarray(mb2), nlev


def _hgrn_kernel(*refs, c, lseq, nch, nlev, has_s0):
    if has_s0:
        q_ref, f_ref, i_ref, lb_ref, a_ref, rq_ref, mb_ref, s0_ref, o_ref, s_ref, st_ref = refs
    else:
        q_ref, f_ref, i_ref, lb_ref, a_ref, rq_ref, mb_ref, o_ref, s_ref, st_ref = refs
    nsq = c // lseq
    lbv = lb_ref[...]
    lane_lo = lax.broadcasted_iota(jnp.int32, (c, LANES), 1) < HEAD_DIM
    r128 = lax.broadcasted_iota(jnp.int32, (LANES, nsq * LANES), 0)
    c128 = lax.broadcasted_iota(jnp.int32, (LANES, nsq * LANES), 1) % LANES
    same_head = (r128 < HEAD_DIM) == (c128 < HEAD_DIM)
    row_seq = lax.broadcasted_iota(jnp.int32, (c, LANES), 0) // lseq
    zero64 = jnp.zeros((HEAD_DIM, HEAD_DIM), F32)

    if has_s0:
        for s in range(nsq):
            top = jnp.concatenate([s0_ref[s, 0], zero64], axis=1)
            bot = jnp.concatenate([zero64, s0_ref[s, 1]], axis=1)
            st_ref[:, s * LANES:(s + 1) * LANES] = jnp.concatenate([top, bot], axis=0).T
    else:
        st_ref[...] = jnp.zeros_like(st_ref)

    def expand(x):
        if nsq == 1:
            return x
        return jnp.concatenate([jnp.where(row_seq == s, x, 0.0) for s in range(nsq)], axis=1)

    def heads_on_rows(x):
        return jnp.concatenate([jnp.where(lane_lo, x, 0.0), jnp.where(lane_lo, 0.0, x)], axis=0)

    def chunk(ci, carry):
        r0 = pl.multiple_of(ci * c, c)
        qa = q_ref[pl.ds(r0, c), :]
        z = f_ref[pl.ds(r0, c), :]
        v = i_ref[pl.ds(r0, c), :]
        lf = jnp.log(lbv + (1.0 - lbv) * jax.nn.sigmoid(z))
        kin = (1.0 - lbv) * jax.nn.sigmoid(-z)
        qs = qa * jax.nn.sigmoid(qa)
        hi, mid, lo = _split3(lf)
        a = a_ref[...]
        g = _dot(a, hi) + _dot(a, mid) + _dot(a, lo)
        eb = jnp.exp(g[0:c])
        qg = qs * eb
        kd = kin * jnp.exp(g[c:2 * c])
        st = st_ref[...]
        o = _dot_nt(expand(qg).astype(BF16), st.astype(BF16))

        kin_b = kin.astype(BF16)
        sc = mb_ref[nlev] * _dot_nt(heads_on_rows(qs).astype(BF16), kin_b)
        for lev in range(nlev):
            y = jnp.exp(g[(2 + lev) * c:(3 + lev) * c])
            isq = rq_ref[lev]
            ql = qs * (y * isq)
            kl = kin * (y * (1.0 - isq))
            sc = sc + mb_ref[lev] * _dot_nt(heads_on_rows(ql).astype(BF16), kl.astype(BF16))
        r = _dot(sc.astype(BF16), v.astype(BF16))
        o_ref[pl.ds(r0, c), :] = o + jnp.where(lane_lo, r[0:c], r[c:2 * c])

        if nsq == 1:
            dl = eb[c - 1:c, :]
        else:
            dl = jnp.concatenate([eb[(s + 1) * lseq - 1:(s + 1) * lseq, :] for s in range(nsq)], axis=1)
        u = _dot(v.T.astype(BF16), expand(kd).astype(BF16))
        st_ref[...] = st * dl + jnp.where(same_head, u, 0.0)
        return carry

    lax.fori_loop(0, nch, chunk, 0)

    for s in range(nsq):
        sbd = st_ref[:, s * LANES:(s + 1) * LANES].T
        s_ref[s, 0] = sbd[0:HEAD_DIM, 0:HEAD_DIM]
        s_ref[s, 1] = sbd[HEAD_DIM:, HEAD_DIM:]


def _hgrn(proj, lb, s0, nseq, t):
    _, n, width = proj.shape
    npair = width // LANES
    c = LANES
    lseq = min(t, c)
    nsq = c // lseq
    if t >= c:
        nch, rows, ngroups = t // c, t, nseq
    else:
        nch, rows, ngroups = 1, c, nseq // nsq
    a, rq, mb, nlev = _hgrn_consts(c, lseq)
    has_s0 = s0 is not None
    ins = [proj, proj, proj, lb.reshape(1, width), a, rq, mb]
    in_specs = [pl.BlockSpec((None, rows, LANES), lambda g, p: (0, g, p)),
                pl.BlockSpec((None, rows, LANES), lambda g, p: (1, g, p)),
                pl.BlockSpec((None, rows, LANES), lambda g, p: (2, g, p)),
                pl.BlockSpec((1, LANES), lambda g, p: (0, p)),
                pl.BlockSpec(a.shape, lambda g, p: (0, 0)),
                pl.BlockSpec(rq.shape, lambda g, p: (0, 0, 0)),
                pl.BlockSpec(mb.shape, lambda g, p: (0, 0, 0))]
    if has_s0:
        ins.append(s0)
        in_specs.append(pl.BlockSpec((nsq, 2, HEAD_DIM, HEAD_DIM), lambda g, p: (g, p, 0, 0)))
    kern = functools.partial(_hgrn_kernel, c=c, lseq=lseq, nch=nch, nlev=nlev, has_s0=has_s0)
    return pl.pallas_call(
        kern,
        out_shape=(jax.ShapeDtypeStruct((n, width), F32),
                   jax.ShapeDtypeStruct((nseq, 2 * npair, HEAD_DIM, HEAD_DIM), F32)),
        grid=(ngroups, npair),
        in_specs=in_specs,
        out_specs=(pl.BlockSpec((rows, LANES), lambda g, p: (g, p)),
                   pl.BlockSpec((nsq, 2, HEAD_DIM, HEAD_DIM), lambda g, p: (g, p, 0, 0))),
        scratch_shapes=[pltpu.VMEM((LANES, nsq * LANES), F32)],
        compiler_params=_params("parallel", "parallel"),
        name="hgrn",
    )(*ins)


def _sb_block(z, tail_in, u2, mask):
    tk = z.shape[1]
    sp = jnp.maximum(z, 0.0) + jnp.log(1.0 + jnp.exp(-jnp.abs(z)))
    ls = -sp
    if mask is not None:
        ls = jnp.where(mask, ls, 0.0)
    hi = ls.astype(BF16)
    lo = (ls - hi.astype(F32)).astype(BF16)
    rr = _dot(hi, u2) + _dot(lo, u2)
    a = jnp.exp(z - sp + tail_in + rr[:, :tk])
    if mask is not None:
        a = jnp.where(mask, a, 0.0)
    return a, tail_in + rr[:, tk:]


def _u2_const(tk):
    j = np.arange(tk)
    u = (j[:, None] > j[None, :]).astype(np.float32)
    return jnp.asarray(np.concatenate([u, np.ones((tk, tk), np.float32)], axis=1), BF16)


def _sb_kernel(q_ref, k_ref, v_ref, bias_ref, u2_ref, o_ref, *, tq, tk):
    qi = pl.program_id(2)
    u2 = u2_ref[...]
    scale = HEAD_DIM ** -0.5
    lane_lo = lax.broadcasted_iota(jnp.int32, (tq, LANES), 1) < HEAD_DIM
    q = q_ref[...] * scale
    rows = qi * tq + lax.broadcasted_iota(jnp.int32, (tq, tk), 0)
    cols = lax.broadcasted_iota(jnp.int32, (tq, tk), 1)
    ndiag = tq // tk
    outs = []
    for h in range(2):
        qh = jnp.where(lane_lo if h == 0 else jnp.logical_not(lane_lo), q, 0.0).astype(BF16)
        bias = bias_ref[2 * pl.program_id(1) + h]

        def block(kb, acc, tail, masked):
            k0 = pl.multiple_of(kb * tk, tk)
            kk = k_ref[pl.ds(k0, tk), :].astype(BF16)
            vv = v_ref[pl.ds(k0, tk), :].astype(BF16)
            z = _dot_nt(qh, kk) + bias
            mask = (cols + kb * tk < rows) if masked else None
            a, tail = _sb_block(z, tail, u2, mask)
            return acc + _dot(a.astype(BF16), vv), tail

        acc = jnp.zeros((tq, LANES), F32)
        tail = jnp.zeros((tq, tk), F32)
        for d in range(ndiag):
            acc, tail = block(qi * ndiag + (ndiag - 1 - d), acc, tail, True)

        def body(i, carry):
            return block(qi * ndiag - 1 - i, carry[0], carry[1], False)

        acc, tail = lax.fori_loop(0, qi * ndiag, body, (acc, tail))
        outs.append(acc)
    o_ref[...] = jnp.where(lane_lo, outs[0], outs[1])


def _sb_prompt(proj, bias, nseq, t, tq, tk):
    _, n, width = proj.shape
    npair = width // LANES
    nq = t // tq
    u2 = _u2_const(tk)
    kern = functools.partial(_sb_kernel, tq=tq, tk=tk)
    return pl.pallas_call(
        kern,
        out_shape=jax.ShapeDtypeStruct((n, width), F32),
        grid=(nseq, npair, nq),
        in_specs=[pl.BlockSpec((None, tq, LANES), lambda b, p, i: (4, b * nq + i, p)),
                  pl.BlockSpec((None, t, LANES), lambda b, p, i: (5, b, p)),
                  pl.BlockSpec((None, t, LANES), lambda b, p, i: (6, b, p)),
                  pl.BlockSpec(memory_space=pltpu.SMEM),
                  pl.BlockSpec(u2.shape, lambda b, p, i: (0, 0))],
        out_specs=pl.BlockSpec((tq, LANES), lambda b, p, i: (b * nq + i, p)),
        compiler_params=_params("parallel", "parallel", "arbitrary"),
        name="sb",
    )(proj, proj, proj, bias, u2)


def _sbpaged_kernel(pt_ref, q_ref, kn_ref, vn_ref, bias_ref, hm_ref, u2_ref, *rest, npages, page, t):
    k_refs = rest[:npages]
    v_refs = rest[npages:2 * npages]
    o_ref = rest[2 * npages]
    nh = q_ref.shape[1] // HEAD_DIM
    rows = nh * t
    width = q_ref.shape[1]
    scale = HEAD_DIM ** -0.5
    u2 = u2_ref[...]
    hm = hm_ref[...]
    qbd = (jnp.concatenate([q_ref[...]] * nh, axis=0) * hm * scale).astype(BF16)
    bias = bias_ref[...]
    acc = jnp.zeros((rows, width), F32)
    tail = jnp.zeros((rows, page), F32)
    pad = jnp.zeros((page - t, width), F32)
    kn = jnp.concatenate([kn_ref[...], pad], axis=0).astype(BF16)
    vn = jnp.concatenate([vn_ref[...], pad], axis=0).astype(BF16)
    qpos = lax.broadcasted_iota(jnp.int32, (rows, page), 0) % t
    kpos = lax.broadcasted_iota(jnp.int32, (rows, page), 1)
    a, tail = _sb_block(_dot_nt(qbd, kn) + bias, tail, u2, kpos < qpos)
    acc = acc + _dot(a.astype(BF16), vn)
    for p in range(npages - 1, -1, -1):
        kk = k_refs[p][...].astype(BF16)
        vv = v_refs[p][...].astype(BF16)
        a, tail = _sb_block(_dot_nt(qbd, kk) + bias, tail, u2, None)
        acc = acc + _dot(a.astype(BF16), vv)
    acc = acc * hm
    out = acc[0:t]
    for h in range(1, nh):
        out = out + acc[h * t:(h + 1) * t]
    o_ref[...] = out


def _sb_paged(proj, cache_k, cache_v, page_table, sb_bias, nseq, t):
    width = proj.shape[-1]
    n = nseq * t
    nh = width // HEAD_DIM
    npages = page_table.shape[1]
    page = cache_k.shape[1]
    rows = nh * t
    u2 = _u2_const(page)
    hm = (np.arange(rows)[:, None] // t == np.arange(width)[None, :] // HEAD_DIM).astype(np.float32)
    bias = jnp.broadcast_to(jnp.repeat(sb_bias.astype(F32), t)[:, None], (rows, page))
    kern = functools.partial(_sbpaged_kernel, npages=npages, page=page, t=t)

    def page_spec(p):
        return pl.BlockSpec((None, page, width), lambda b, pt: (pt[b, p], 0, 0))

    grid_spec = pltpu.PrefetchScalarGridSpec(
        num_scalar_prefetch=1,
        grid=(nseq,),
        in_specs=[pl.BlockSpec((None, None, t, width), lambda b, pt: (4, b, 0, 0)),
                  pl.BlockSpec((None, None, t, width), lambda b, pt: (5, b, 0, 0)),
                  pl.BlockSpec((None, None, t, width), lambda b, pt: (6, b, 0, 0)),
                  pl.BlockSpec((rows, page), lambda b, pt: (0, 0)),
                  pl.BlockSpec((rows, width), lambda b, pt: (0, 0)),
                  pl.BlockSpec(u2.shape, lambda b, pt: (0, 0))]
                 + [page_spec(p) for p in range(npages)] * 2,
        out_specs=pl.BlockSpec((t, width), lambda b, pt: (b, 0)),
    )
    return pl.pallas_call(
        kern,
        out_shape=jax.ShapeDtypeStruct((n, width), F32),
        grid_spec=grid_spec,
        compiler_params=_params("arbitrary"),
        name="sbpaged",
    )(page_table, proj, proj, proj, bias, jnp.asarray(hm), u2,
      *([cache_k] * npages), *([cache_v] * npages))


def _layernorm_rows(x, g, b):
    mu = jnp.mean(x, axis=-1, keepdims=True)
    xc = x - mu
    var = jnp.mean(xc * xc, axis=-1, keepdims=True)
    return xc * lax.rsqrt(var + LN_EPS) * g + b


def _head_rms(o, hsum):
    sq = o * o
    hi = sq.astype(BF16)
    lo = (sq - hi.astype(F32)).astype(BF16)
    ms = (_dot(hi, hsum) + _dot(lo, hsum)) * (1.0 / HEAD_DIM)
    return o * lax.rsqrt(ms + RMS_EPS)


def _outproj_kernel(oa_ref, ga_ref, ob_ref, x_ref, g1_ref, sh2_ref, sc2_ref, na_ref, nb_ref, hs_ref,
                    w_ref, lg_ref, lb_ref, x1_ref, h2t_ref, *, alpha):
    hs = hs_ref[...]
    ga = ga_ref[...]
    ya = _head_rms(oa_ref[...], hs) * na_ref[...] * (ga * jax.nn.sigmoid(ga))
    yb = _head_rms(ob_ref[...], hs) * nb_ref[...]
    wa = w_ref.shape[0] // 2
    mix = _dot(ya.astype(BF16), w_ref[0:wa, :]) + _dot(yb.astype(BF16), w_ref[wa:, :])
    x1 = _layernorm_rows(alpha * x_ref[...] + (1.0 + g1_ref[...]) * mix, lg_ref[...], lb_ref[...])
    x1_ref[...] = x1
    h2 = x1 * (1.0 + sc2_ref[...]) + sh2_ref[...]
    h2t_ref[...] = h2.T.astype(BF16)


def _outproj(oa, proj, ob, x, ms, norm_a, norm_b, w_out_bf16, ln_g, ln_b, alpha):
    n, d = x.shape
    tm = ms.tm
    wa = oa.shape[1]
    hs = (np.arange(wa)[:, None] // HEAD_DIM == np.arange(wa)[None, :] // HEAD_DIM).astype(np.float32)
    row = lambda i: (i, 0)
    fixed = lambda i: (0, 0)
    return pl.pallas_call(
        functools.partial(_outproj_kernel, alpha=alpha),
        out_shape=(jax.ShapeDtypeStruct((n, d), F32), jax.ShapeDtypeStruct((d, n), BF16)),
        grid=(n // tm,),
        in_specs=[pl.BlockSpec((tm, wa), row),
                  pl.BlockSpec((None, tm, wa), lambda i: (3, i, 0)),
                  pl.BlockSpec((tm, wa), row),
                  pl.BlockSpec((tm, d), row),
                  ms.spec(2, 1), ms.spec(3, 1), ms.spec(4, 1),
                  pl.BlockSpec((1, wa), fixed), pl.BlockSpec((1, wa), fixed),
                  pl.BlockSpec((wa, wa), fixed),
                  pl.BlockSpec((2 * wa, d), fixed),
                  pl.BlockSpec((1, d), fixed), pl.BlockSpec((1, d), fixed)],
        out_specs=(pl.BlockSpec((tm, d), row), pl.BlockSpec((d, tm), lambda i: (0, i))),
        compiler_params=_params("parallel"),
        name="outproj",
    )(oa, proj, ob, x, ms.arr, ms.arr, ms.arr, norm_a.reshape(1, wa), norm_b.reshape(1, wa),
      jnp.asarray(hs, BF16), w_out_bf16, ln_g.reshape(1, d), ln_b.reshape(1, d))


def _stair_counts():
    return [PEER_TOPK // (a + 1) for a in range(PEER_TOPK)]


def _extract_topk(s, k):
    rows, tn = s.shape
    ridx = lax.broadcasted_iota(jnp.int32, (rows, tn), 0)
    kidx = lax.broadcasted_iota(jnp.int32, (k, tn), 0)

    def body(i, carry):
        cur, vals, rank = carry
        m = jnp.max(cur, axis=0, keepdims=True)
        first = jnp.min(jnp.where(cur == m, ridx, rows), axis=0, keepdims=True)
        hit = ridx == first
        cur = jnp.where(hit, -jnp.inf, cur)
        rank = jnp.where(hit, i.astype(F32), rank)
        vals = jnp.where(kidx == i, m, vals)
        return cur, vals, rank

    init = (s, jnp.zeros((k, tn), F32), jnp.full((rows, tn), float(k), F32))
    cur, vals, rank = lax.fori_loop(0, k, body, init)
    return vals, rank, cur


def _peertopk_kernel(ht_ref, wqt_ref, k1_ref, k2_ref, c1_ref, a1_ref, r2_ref, a2_ref, v1_scr, *, nheads, nkeys):
    tn = ht_ref.shape[1]
    dq = k1_ref.shape[1]
    k = PEER_TOPK
    stair = _stair_counts()
    qt = _dot(wqt_ref[...], ht_ref[...]).astype(BF16)
    k1 = k1_ref[...].astype(BF16)
    k2 = k2_ref[...].astype(BF16)
    b8 = lax.broadcasted_iota(jnp.int32, (8, tn), 0)
    b16 = lax.broadcasted_iota(jnp.int32, (k, tn), 0)
    for h in range(nheads):
        s1 = _dot(k1, qt[(2 * h) * dq:(2 * h + 1) * dq, :])
        s2 = _dot(k2, qt[(2 * h + 1) * dq:(2 * h + 2) * dq, :])
        v1, rank1, _ = _extract_topk(s1, k)
        v2, rank2, _ = _extract_topk(s2, k)
        v1_scr[...] = v1
        pieces = []
        for a in range(k):
            va = jnp.broadcast_to(v1_scr[pl.ds(a, 1), :], (8 if stair[a] <= 8 else k, tn))
            if stair[a] <= 8:
                pieces.append(jnp.where(b8 < stair[a], va + v2[0:8], -jnp.inf))
            else:
                pieces.append(va + v2)
        cand = jnp.concatenate(pieces, axis=0)
        top, _, knocked = _extract_topk(cand, k)
        zsum = jnp.sum(jnp.exp(top - top[0:1]), axis=0, keepdims=True)
        inv_z = 1.0 / zsum
        taken = (knocked == -jnp.inf) & (cand > -jnp.inf)
        c1 = jnp.zeros((nkeys, tn), F32)
        off = 0
        for a in range(k):
            nr = 8 if stair[a] <= 8 else k
            cnt = jnp.sum(taken[off:off + nr].astype(F32), axis=0, keepdims=True)
            c1 = jnp.where(rank1 == float(a), cnt, c1)
            off += nr
        a1 = jnp.where(rank1 < float(k), jnp.exp(s1 - v1[0:1]) * inv_z, 0.0)
        a2 = jnp.where(rank2 < float(k), jnp.exp(s2 - v2[0:1]), 0.0)
        c1_ref[h] = c1
        a1_ref[h] = a1
        r2_ref[h] = rank2.astype(BF16)
        a2_ref[h] = a2.astype(BF16)


def _peer_topk(h2t, wqt_bf16, k1, k2, tn):
    d, n = h2t.shape
    nkeys, dq = k1.shape
    nheads = wqt_bf16.shape[0] // (2 * dq)
    shp = (nheads, nkeys, n)
    blk = pl.BlockSpec((nheads, nkeys, tn), lambda i: (0, 0, i))
    return pl.pallas_call(
        functools.partial(_peertopk_kernel, nheads=nheads, nkeys=nkeys),
        out_shape=(jax.ShapeDtypeStruct(shp, F32), jax.ShapeDtypeStruct(shp, F32),
                   jax.ShapeDtypeStruct(shp, BF16), jax.ShapeDtypeStruct(shp, BF16)),
        grid=(n // tn,),
        in_specs=[pl.BlockSpec((d, tn), lambda i: (0, i)),
                  pl.BlockSpec(wqt_bf16.shape, lambda i: (0, 0)),
                  pl.BlockSpec((nkeys, dq), lambda i: (0, 0)),
                  pl.BlockSpec((nkeys, dq), lambda i: (0, 0))],
        out_specs=(blk, blk, blk, blk),
        scratch_shapes=[pltpu.VMEM((PEER_TOPK, tn), F32)],
        compiler_params=_params("parallel"),
        name="peertopk",
    )(h2t, wqt_bf16, k1, k2)


def _gelu_tanh(x):
    return 0.5 * x * (1.0 + jnp.tanh(0.7978845608028654 * (x + 0.044715 * (x * x * x))))


def _peermain_kernel(ht_ref, u_ref, vt_ref, c1_ref, a1_ref, r2_ref, a2_ref, o_ref, act_ref, g_ref, *, nkeys, nheads):
    j = pl.program_id(1)
    te = u_ref.shape[0]
    tn = ht_ref.shape[1]
    nsub = te // nkeys

    @pl.when(j == 0)
    def _():
        o_ref[...] = jnp.zeros_like(o_ref)

    act_ref[...] = _dot(u_ref[...], ht_ref[...])
    for sub in range(nsub):
        i1 = j * nsub + sub
        w = jnp.zeros((nkeys, tn), BF16)
        for h in range(nheads):
            c1 = jnp.broadcast_to(c1_ref[h, pl.ds(i1, 1), :], (nkeys, tn)).astype(BF16)
            a1 = jnp.broadcast_to(a1_ref[h, pl.ds(i1, 1), :], (nkeys, tn)).astype(BF16)
            w = w + a1 * jnp.where(r2_ref[h] < c1, a2_ref[h], jnp.zeros((), BF16))
        act = act_ref[sub * nkeys:(sub + 1) * nkeys, :]
        g_ref[sub * nkeys:(sub + 1) * nkeys, :] = w * _gelu_tanh(act).astype(BF16)
    o_ref[...] += _dot(vt_ref[...], g_ref[...])


def _peer_main(h2t, u_bf16, vt_bf16, c1, a1, r2, a2, tn, te):
    d, n = h2t.shape
    ne = u_bf16.shape[0]
    nheads, nkeys, _ = c1.shape
    small = pl.BlockSpec((nheads, nkeys, tn), lambda i, j: (0, 0, i))
    return pl.pallas_call(
        functools.partial(_peermain_kernel, nkeys=nkeys, nheads=nheads),
        out_shape=jax.ShapeDtypeStruct((d, n), F32),
        grid=(n // tn, ne // te),
        in_specs=[pl.BlockSpec((d, tn), lambda i, j: (0, i)),
                  pl.BlockSpec((te, d), lambda i, j: (j, 0)),
                  pl.BlockSpec((d, te), lambda i, j: (0, j)),
                  small, small, small, small],
        out_specs=pl.BlockSpec((d, tn), lambda i, j: (0, i)),
        scratch_shapes=[pltpu.VMEM((te, tn), F32), pltpu.VMEM((te, tn), BF16)],
        compiler_params=_params("parallel", "arbitrary"),
        name="peermain",
    )(h2t, u_bf16, vt_bf16, c1, a1, r2, a2)


def _ln2_kernel(x1_ref, fft_ref, g2_ref, lg_ref, lb_ref, y_ref, *, alpha):
    ff = fft_ref[...].T
    y_ref[...] = _layernorm_rows(alpha * x1_ref[...] + (1.0 + g2_ref[...]) * ff, lg_ref[...], lb_ref[...])


def _ln2(x1, fft, col0, ms, ln_g, ln_b, alpha):
    n, d = x1.shape
    tm = ms.tm
    c0 = col0 // tm
    return pl.pallas_call(
        functools.partial(_ln2_kernel, alpha=alpha),
        out_shape=jax.ShapeDtypeStruct((n, d), F32),
        grid=(n // tm,),
        in_specs=[pl.BlockSpec((tm, d), lambda i: (i, 0)),
                  pl.BlockSpec((d, tm), lambda i: (0, c0 + i)),
                  ms.spec(5, 1),
                  pl.BlockSpec((1, d), lambda i: (0, 0)), pl.BlockSpec((1, d), lambda i: (0, 0))],
        out_specs=pl.BlockSpec((tm, d), lambda i: (i, 0)),
        compiler_params=_params("parallel"),
        name="ln2",
    )(x1, fft, ms.arr, ln_g.reshape(1, d), ln_b.reshape(1, d))


def _pick(n, pref):
    for t in pref:
        if n % t == 0:
            return t
    raise ValueError(f"no tile for {n}")


def kernel(x_prompt, x_sample, cache_k, cache_v, state_hgrn, page_table, c_prompt, c_sample, w_ada, b_ada, w_in, lb_raw, norm_a, norm_b, sb_bias, w_out, ln1_g, ln1_b, ln2_g, ln2_b, peer_wq, peer_k1, peer_k2, peer_u, peer_v):
    depth = w_ada.shape[0]
    bp, tp, d = x_prompt.shape
    bs, ts, _ = x_sample.shape
    n_p, n_s = bp * tp, bs * ts
    page, hb = cache_k.shape[2], cache_k.shape[3]
    wb = hb * HEAD_DIM
    alpha = (2.0 * depth) ** 0.25
    lb_all = jnp.cumsum(jax.nn.softmax(lb_raw.astype(F32), axis=0), axis=0)

    tm_p = _pick(tp, (1024, 512, 256, 128))
    tm_s = _pick(n_s, (1024, 512, 256, 128))
    assert n_p % tm_s == 0
    n_all = n_p + n_s
    tn_k = _pick(n_all, (256, 128))
    tn_m = _pick(n_all, (512, 256, 128))
    tq = _pick(tp, (256, 128))

    yp = x_prompt.reshape(n_p, d)
    ys = x_sample.reshape(n_s, d)
    outs = {k: [] for k in ("kp", "vp", "sp", "ks", "vs", "ss")}
    for l in range(depth):
        mod = _mod(jnp.concatenate([c_prompt, c_sample], axis=0), w_ada[l], b_ada[l])
        ms_p = _ModSpec(mod[:bp].reshape(bp, 1, 6 * d), False, tm_p, tp // tm_p, d)
        ms_s = _ModSpec(jnp.repeat(mod[bp:], ts, axis=0), True, tm_s, 1, d)
        w_in_b = w_in[l].astype(BF16)
        w_out_b = w_out[l].astype(BF16)
        lb = lb_all[l]

        proj_p = _inproj(yp, ms_p, w_in_b, wb)
        proj_s = _inproj(ys, ms_s, w_in_b, wb)

        oa_p, sp = _hgrn(proj_p, lb, None, bp, tp)
        oa_s, ss = _hgrn(proj_s, lb, state_hgrn[l], bs, ts)

        ob_p = _sb_prompt(proj_p, sb_bias[l].astype(F32), bp, tp, tq, LANES)
        ck = cache_k[l].reshape(cache_k.shape[1], page, wb)
        cv = cache_v[l].reshape(cache_v.shape[1], page, wb)
        ob_s = _sb_paged(proj_s.reshape(proj_s.shape[0], bs, ts, wb), ck, cv, page_table, sb_bias[l], bs, ts)
        ob_s = ob_s.reshape(n_s, wb)

        x1_p, h2t_p = _outproj(oa_p, proj_p, ob_p, yp, ms_p, norm_a[l], norm_b[l], w_out_b, ln1_g[l], ln1_b[l], alpha)
        x1_s, h2t_s = _outproj(oa_s, proj_s, ob_s, ys, ms_s, norm_a[l], norm_b[l], w_out_b, ln1_g[l], ln1_b[l], alpha)

        h2t = jnp.concatenate([h2t_p, h2t_s], axis=1)
        c1, a1, r2, a2 = _peer_topk(h2t, peer_wq[l].T.astype(BF16), peer_k1[l], peer_k2[l], tn_k)
        fft = _peer_main(h2t, peer_u[l].astype(BF16), peer_v[l].T.astype(BF16), c1, a1, r2, a2,
                         tn_m, 8 * peer_k1.shape[1])

        yp = _ln2(x1_p, fft, 0, ms_p, ln2_g[l], ln2_b[l], alpha)
        ys = _ln2(x1_s, fft, n_p, ms_s, ln2_g[l], ln2_b[l], alpha)

        outs["kp"].append(proj_p[5].reshape(bp, tp, hb, HEAD_DIM))
        outs["vp"].append(proj_p[6].reshape(bp, tp, hb, HEAD_DIM))
        outs["sp"].append(sp)
        outs["ks"].append(proj_s[5].reshape(bs, ts, hb, HEAD_DIM))
        outs["vs"].append(proj_s[6].reshape(bs, ts, hb, HEAD_DIM))
        outs["ss"].append(ss)
    return (yp.reshape(bp, tp, d), ys.reshape(bs, ts, d),
            jnp.stack(outs["kp"]), jnp.stack(outs["vp"]), jnp.stack(outs["sp"]),
            jnp.stack(outs["ks"]), jnp.stack(outs["vs"]), jnp.stack(outs["ss"]))
```

```python
import functools
import math

import numpy as np
import jax
import jax.numpy as jnp
from jax import lax
from jax.experimental import pallas as pl
from jax.experimental.pallas import tpu as pltpu

F32 = jnp.float32
BF16 = jnp.bfloat16

HEAD_DIM = 64
LANES = 128
LN_EPS = 1e-5
RMS_EPS = 1e-6
PEER_TOPK = 16
VMEM_LIMIT = 56 * 1024 * 1024

_NT = (((1,), (1,)), ((), ()))
_TN = (((0,), (0,)), ((), ()))


def _dot(a, b):
    return jnp.dot(a, b, preferred_element_type=F32)


def _dot_nt(a, b):
    return lax.dot_general(a, b, _NT, preferred_element_type=F32)


def _split3(x):
    hi = x.astype(BF16)
    r1 = x - hi.astype(F32)
    mid = r1.astype(BF16)
    lo = (r1 - mid.astype(F32)).astype(BF16)
    return hi, mid, lo


def _params(*sem):
    return pltpu.CompilerParams(dimension_semantics=sem, vmem_limit_bytes=VMEM_LIMIT)


def _mod_kernel(c_ref, w_ref, b_ref, o_ref):
    o_ref[...] = _dot(c_ref[...].astype(BF16), w_ref[...].astype(BF16)) + b_ref[...]


def _mod(c, w, b):
    n, d = c.shape
    width = w.shape[1]
    tn = 1024
    return pl.pallas_call(
        _mod_kernel,
        out_shape=jax.ShapeDtypeStruct((n, width), F32),
        grid=(width // tn,),
        in_specs=[pl.BlockSpec((n, d), lambda j: (0, 0)),
                  pl.BlockSpec((d, tn), lambda j: (0, j)),
                  pl.BlockSpec((1, tn), lambda j: (0, j))],
        out_specs=pl.BlockSpec((n, tn), lambda j: (0, j)),
        compiler_params=_params("parallel"),
        name="mod",
    )(c, w, b.reshape(1, width))


class _ModSpec:
    def __init__(self, arr, per_token, tm, tiles_per_seq, d):
        self.arr, self.per_token, self.tm, self.tps, self.d = arr, per_token, tm, tiles_per_seq, d

    def spec(self, part, ngrid):
        d, tps = self.d, self.tps
        if self.per_token:
            if ngrid == 1:
                return pl.BlockSpec((self.tm, d), lambda i: (i, part))
            return pl.BlockSpec((self.tm, d), lambda i, j: (i, part))
        if ngrid == 1:
            return pl.BlockSpec((None, 1, d), lambda i: (i // tps, 0, part))
        return pl.BlockSpec((None, 1, d), lambda i, j: (i // tps, 0, part))


def _inproj_kernel(x_ref, sh_ref, sc_ref, w_ref, o_ref, h_ref):
    @pl.when(pl.program_id(1) == 0)
    def _():
        h_ref[...] = (x_ref[...] * (1.0 + sc_ref[...]) + sh_ref[...]).astype(BF16)

    o_ref[...] = _dot(h_ref[...], w_ref[...])


def _inproj(x, ms, w_bf16, tn):
    n, d = x.shape
    tm = ms.tm
    nparts = w_bf16.shape[1] // tn
    return pl.pallas_call(
        _inproj_kernel,
        out_shape=jax.ShapeDtypeStruct((nparts, n, tn), F32),
        grid=(n // tm, nparts),
        in_specs=[pl.BlockSpec((tm, d), lambda i, j: (i, 0)),
                  ms.spec(0, 2), ms.spec(1, 2),
                  pl.BlockSpec((d, tn), lambda i, j: (0, j))],
        out_specs=pl.BlockSpec((None, tm, tn), lambda i, j: (j, i, 0)),
        scratch_shapes=[pltpu.VMEM((tm, d), BF16)],
        compiler_params=_params("parallel", "arbitrary"),
        name="inproj",
    )(x, ms.arr, ms.arr, w_bf16)


def _hgrn_consts(c, lseq):
    nlev = int(math.log2(lseq))
    r = np.arange(c)
    seg, pos = r // lseq, r % lseq
    j = np.arange(c)
    same = seg[:, None] == seg[None, :]
    mats = [same & (j[None, :] <= r[:, None]),
            same & (j[None, :] > r[:, None])]
    rq = np.zeros((nlev, c, 1), np.float32)
    mb = np.zeros((nlev + 1, c, c), np.float32)
    for lev in range(nlev):
        half, blk = 1 << lev, 2 << lev
        is_q = (pos % blk) >= half
        ref = (r // blk) * blk + half - 1
        m = np.where(is_q[:, None],
                     (j[None, :] > ref[:, None]) & (j[None, :] <= r[:, None]),
                     (j[None, :] > r[:, None]) & (j[None, :] <= ref[:, None]))
        mats.append(m)
        rq[lev, :, 0] = is_q
        mb[lev] = (r[:, None] // blk) == (r[None, :] // blk)
    mb[nlev] = np.eye(c)
    a = np.concatenate(mats, axis=0).astype(np.float32)
    rq = np.broadcast_to(rq, (nlev, c, LANES)).copy()
    mb2 = np.concatenate([mb, mb], axis=1)
    return jnp.asarray(a, BF16), jnp.asarray(rq), jnp.asarray(mb2), nlev


def _hgrn_kernel(*refs, c, lseq, nch, nlev, has_s0):
    if has_s0:
        q_ref, f_ref, i_ref, lb_ref, a_ref, rq_ref, mb_ref, s0_ref, o_ref, s_ref, st_ref = refs
    else:
        q_ref, f_ref, i_ref, lb_ref, a_ref, rq_ref, mb_ref, o_ref, s_ref, st_ref = refs
    nsq = c // lseq
    lbv = lb_ref[...]
    lane_lo = lax.broadcasted_iota(jnp.int32, (c, LANES), 1) < HEAD_DIM
    r128 = lax.broadcasted_iota(jnp.int32, (LANES, nsq * LANES), 0)
    c128 = lax.broadcasted_iota(jnp.int32, (LANES, nsq * LANES), 1) % LANES
    same_head = (r128 < HEAD_DIM) == (c128 < HEAD_DIM)
    row_seq = lax.broadcasted_iota(jnp.int32, (c, LANES), 0) // lseq
    zero64 = jnp.zeros((HEAD_DIM, HEAD_DIM), F32)

    if has_s0:
        for s in range(nsq):
            top = jnp.concatenate([s0_ref[s, 0], zero64], axis=1)
            bot = jnp.concatenate([zero64, s0_ref[s, 1]], axis=1)
            st_ref[:, s * LANES:(s + 1) * LANES] = jnp.concatenate([top, bot], axis=0).T
    else:
        st_ref[...] = jnp.zeros_like(st_ref)

    def expand(x):
        if nsq == 1:
            return x
        return jnp.concatenate([jnp.where(row_seq == s, x, 0.0) for s in range(nsq)], axis=1)

    def heads_on_rows(x):
        return jnp.concatenate([jnp.where(lane_lo, x, 0.0), jnp.where(lane_lo, 0.0, x)], axis=0)

    def chunk(ci, carry):
        r0 = pl.multiple_of(ci * c, c)
        qa = q_ref[pl.ds(r0, c), :]
        z = f_ref[pl.ds(r0, c), :]
        v = i_ref[pl.ds(r0, c), :]
        lf = jnp.log(lbv + (1.0 - lbv) * jax.nn.sigmoid(z))
        kin = (1.0 - lbv) * jax.nn.sigmoid(-z)
        qs = qa * jax.nn.sigmoid(qa)
        hi, mid, lo = _split3(lf)
        a = a_ref[...]
        g = _dot(a, hi) + _dot(a, mid) + _dot(a, lo)
        eb = jnp.exp(g[0:c])
        qg = qs * eb
        kd = kin * jnp.exp(g[c:2 * c])
        st = st_ref[...]
        o = _dot_nt(expand(qg).astype(BF16), st.astype(BF16))

        kin_b = kin.astype(BF16)
        sc = mb_ref[nlev] * _dot_nt(heads_on_rows(qs).astype(BF16), kin_b)
        for lev in range(nlev):
            y = jnp.exp(g[(2 + lev) * c:(3 + lev) * c])
            isq = rq_ref[lev]
            ql = qs * (y * isq)
            kl = kin * (y * (1.0 - isq))
            sc = sc + mb_ref[lev] * _dot_nt(heads_on_rows(ql).astype(BF16), kl.astype(BF16))
        r = _dot(sc.astype(BF16), v.astype(BF16))
        o_ref[pl.ds(r0, c), :] = o + jnp.where(lane_lo, r[0:c], r[c:2 * c])

        if nsq == 1:
            dl = eb[c - 1:c, :]
        else:
            dl = jnp.concatenate([eb[(s + 1) * lseq - 1:(s + 1) * lseq, :] for s in range(nsq)], axis=1)
        u = _dot(v.T.astype(BF16), expand(kd).astype(BF16))
        st_ref[...] = st * dl + jnp.where(same_head, u, 0.0)
        return carry

    lax.fori_loop(0, nch, chunk, 0)

    for s in range(nsq):
        sbd = st_ref[:, s * LANES:(s + 1) * LANES].T
        s_ref[s, 0] = sbd[0:HEAD_DIM, 0:HEAD_DIM]
        s_ref[s, 1] = sbd[HEAD_DIM:, HEAD_DIM:]


def _hgrn(proj, lb, s0, nseq, t):
    _, n, width = proj.shape
    npair = width // LANES
    c = LANES
    lseq = min(t, c)
    nsq = c // lseq
    if t >= c:
        nch, rows, ngroups = t // c, t, nseq
    else:
        nch, rows, ngroups = 1, c, nseq // nsq
    a, rq, mb, nlev = _hgrn_consts(c, lseq)
    has_s0 = s0 is not None
    ins = [proj, proj, proj, lb.reshape(1, width), a, rq, mb]
    in_specs = [pl.BlockSpec((None, rows, LANES), lambda g, p: (0, g, p)),
                pl.BlockSpec((None, rows, LANES), lambda g, p: (1, g, p)),
                pl.BlockSpec((None, rows, LANES), lambda g, p: (2, g, p)),
                pl.BlockSpec((1, LANES), lambda g, p: (0, p)),
                pl.BlockSpec(a.shape, lambda g, p: (0, 0)),
                pl.BlockSpec(rq.shape, lambda g, p: (0, 0, 0)),
                pl.BlockSpec(mb.shape, lambda g, p: (0, 0, 0))]
    if has_s0:
        ins.append(s0)
        in_specs.append(pl.BlockSpec((nsq, 2, HEAD_DIM, HEAD_DIM), lambda g, p: (g, p, 0, 0)))
    kern = functools.partial(_hgrn_kernel, c=c, lseq=lseq, nch=nch, nlev=nlev, has_s0=has_s0)
    return pl.pallas_call(
        kern,
        out_shape=(jax.ShapeDtypeStruct((n, width), F32),
                   jax.ShapeDtypeStruct((nseq, 2 * npair, HEAD_DIM, HEAD_DIM), F32)),
        grid=(ngroups, npair),
        in_specs=in_specs,
        out_specs=(pl.BlockSpec((rows, LANES), lambda g, p: (g, p)),
                   pl.BlockSpec((nsq, 2, HEAD_DIM, HEAD_DIM), lambda g, p: (g, p, 0, 0))),
        scratch_shapes=[pltpu.VMEM((LANES, nsq * LANES), F32)],
        compiler_params=_params("parallel", "parallel"),
        name="hgrn",
    )(*ins)


LOG2E = 1.4426950408889634


def _sb_block(z2, tail, uneg, mask):
    sp = jnp.maximum(z2, 0.0) + jnp.log2(1.0 + jnp.exp2(-jnp.abs(z2)))
    if mask is not None:
        sp = jnp.where(mask, sp, 0.0)
    rr = _dot(sp.astype(BF16), uneg)
    a = jnp.exp2(z2 - sp + (tail + rr))
    if mask is not None:
        a = jnp.where(mask, a, 0.0)
    return a, tail - jnp.sum(sp, axis=1, keepdims=True)


def _uneg_const(tk):
    j = np.arange(tk)
    return jnp.asarray(-(j[:, None] > j[None, :]).astype(np.float32), BF16)


def _sb_kernel(q_ref, k_ref, v_ref, bias_ref, u2_ref, o_ref, *, tq, tk):
    qi = pl.program_id(2)
    u2 = u2_ref[...]
    scale = HEAD_DIM ** -0.5 * LOG2E
    lane_lo = lax.broadcasted_iota(jnp.int32, (tq, LANES), 1) < HEAD_DIM
    q = q_ref[...] * scale
    rows = qi * tq + lax.broadcasted_iota(jnp.int32, (tq, tk), 0)
    cols = lax.broadcasted_iota(jnp.int32, (tq, tk), 1)
    ndiag = tq // tk
    qh = (jnp.where(lane_lo, q, 0.0).astype(BF16), jnp.where(lane_lo, 0.0, q).astype(BF16))
    bias = (bias_ref[2 * pl.program_id(1)] * LOG2E, bias_ref[2 * pl.program_id(1) + 1] * LOG2E)

    def block(kb, carry, masked):
        k0 = pl.multiple_of(kb * tk, tk)
        kk = k_ref[pl.ds(k0, tk), :].astype(BF16)
        vv = v_ref[pl.ds(k0, tk), :].astype(BF16)
        mask = (cols + kb * tk < rows) if masked else None
        out = []
        for h in range(2):
            acc, tail = carry[h]
            a, tail = _sb_block(_dot_nt(qh[h], kk) + bias[h], tail, u2, mask)
            out.append((acc + _dot(a.astype(BF16), vv), tail))
        return tuple(out)

    zero = (jnp.zeros((tq, LANES), F32), jnp.zeros((tq, 1), F32))
    carry = (zero, zero)
    for d in range(ndiag):
        carry = block(qi * ndiag + (ndiag - 1 - d), carry, True)
    carry = lax.fori_loop(0, qi * ndiag, lambda i, c: block(qi * ndiag - 1 - i, c, False), carry)
    o_ref[...] = jnp.where(lane_lo, carry[0][0], carry[1][0])


def _sb_prompt(proj, bias, nseq, t, tq, tk):
    _, n, width = proj.shape
    npair = width // LANES
    nq = t // tq
    u2 = _uneg_const(tk)
    kern = functools.partial(_sb_kernel, tq=tq, tk=tk)
    return pl.pallas_call(
        kern,
        out_shape=jax.ShapeDtypeStruct((n, width), F32),
        grid=(nseq, npair, nq),
        in_specs=[pl.BlockSpec((None, tq, LANES), lambda b, p, i: (4, b * nq + i, p)),
                  pl.BlockSpec((None, t, LANES), lambda b, p, i: (5, b, p)),
                  pl.BlockSpec((None, t, LANES), lambda b, p, i: (6, b, p)),
                  pl.BlockSpec(memory_space=pltpu.SMEM),
                  pl.BlockSpec(u2.shape, lambda b, p, i: (0, 0))],
        out_specs=pl.BlockSpec((tq, LANES), lambda b, p, i: (b * nq + i, p)),
        compiler_params=_params("parallel", "parallel", "arbitrary"),
        name="sb",
    )(proj, proj, proj, bias, u2)


def _sbpaged_kernel(pt_ref, q_ref, kn_ref, vn_ref, bias_ref, u2_ref, *rest, npages, page, t, nh):
    k_refs = rest[:npages]
    v_refs = rest[npages:2 * npages]
    o_ref = rest[2 * npages]
    rows = nh * t
    scale = HEAD_DIM ** -0.5 * LOG2E
    u2 = u2_ref[...]
    qa = (q_ref[...] * scale).astype(BF16)
    bias = bias_ref[...]
    head_z = lax.broadcasted_iota(jnp.int32, (rows, page), 0) // t
    head_o = lax.broadcasted_iota(jnp.int32, (rows, HEAD_DIM), 0) // t

    def logits(k_of_head):
        z = bias
        for h in range(nh):
            z = z + jnp.where(head_z == h, _dot_nt(qa, k_of_head(h).astype(BF16)), 0.0)
        return z

    def mix(a, v_of_head):
        ab = a.astype(BF16)
        o = jnp.zeros((rows, HEAD_DIM), F32)
        for h in range(nh):
            o = o + jnp.where(head_o == h, _dot(ab, v_of_head(h).astype(BF16)), 0.0)
        return o

    pad = jnp.zeros((page - t, HEAD_DIM), F32)
    qpos = lax.broadcasted_iota(jnp.int32, (rows, page), 0) % t
    kpos = lax.broadcasted_iota(jnp.int32, (rows, page), 1)
    z = logits(lambda h: jnp.concatenate([kn_ref[h], pad], axis=0))
    a, tail = _sb_block(z, jnp.zeros((rows, 1), F32), u2, kpos < qpos)
    acc = mix(a, lambda h: jnp.concatenate([vn_ref[h], pad], axis=0))
    for p in range(npages - 1, -1, -1):
        k_ref, v_ref = k_refs[p], v_refs[p]
        z = logits(lambda h: k_ref[pl.ds(h, page, stride=nh), :])
        a, tail = _sb_block(z, tail, u2, None)
        acc = acc + mix(a, lambda h: v_ref[pl.ds(h, page, stride=nh), :])
    o_ref[...] = acc


def _sb_paged(q, kn, vn, cache_k, cache_v, page_table, sb_bias, page):
    nseq, nh, t, _ = kn.shape
    npages = page_table.shape[1]
    rows = nh * t
    u2 = _uneg_const(page)
    bias = jnp.broadcast_to(jnp.repeat(sb_bias.astype(F32) * LOG2E, t)[:, None], (rows, page))
    kern = functools.partial(_sbpaged_kernel, npages=npages, page=page, t=t, nh=nh)

    def page_spec(p):
        return pl.BlockSpec((None, page * nh, HEAD_DIM), lambda b, pt: (pt[b, p], 0, 0))

    grid_spec = pltpu.PrefetchScalarGridSpec(
        num_scalar_prefetch=1,
        grid=(nseq,),
        in_specs=[pl.BlockSpec((None, rows, HEAD_DIM), lambda b, pt: (b, 0, 0)),
                  pl.BlockSpec((None, nh, t, HEAD_DIM), lambda b, pt: (b, 0, 0, 0)),
                  pl.BlockSpec((None, nh, t, HEAD_DIM), lambda b, pt: (b, 0, 0, 0)),
                  pl.BlockSpec((rows, page), lambda b, pt: (0, 0)),
                  pl.BlockSpec(u2.shape, lambda b, pt: (0, 0))]
                 + [page_spec(p) for p in range(npages)] * 2,
        out_specs=pl.BlockSpec((None, rows, HEAD_DIM), lambda b, pt: (b, 0, 0)),
    )
    return pl.pallas_call(
        kern,
        out_shape=jax.ShapeDtypeStruct((nseq, rows, HEAD_DIM), F32),
        grid_spec=grid_spec,
        compiler_params=_params("arbitrary"),
        name="sbpaged",
    )(page_table, q, kn, vn, bias, u2, *([cache_k] * npages), *([cache_v] * npages))


def _layernorm_rows(x, g, b):
    mu = jnp.mean(x, axis=-1, keepdims=True)
    xc = x - mu
    var = jnp.mean(xc * xc, axis=-1, keepdims=True)
    return xc * lax.rsqrt(var + LN_EPS) * g + b


def _head_rms(o, hsum):
    sq = o * o
    hi = sq.astype(BF16)
    lo = (sq - hi.astype(F32)).astype(BF16)
    ms = (_dot(hi, hsum) + _dot(lo, hsum)) * (1.0 / HEAD_DIM)
    return o * lax.rsqrt(ms + RMS_EPS)


def _outproj_kernel(oa_ref, ga_ref, ob_ref, x_ref, g1_ref, sh2_ref, sc2_ref, na_ref, nb_ref, hs_ref,
                    w_ref, lg_ref, lb_ref, x1_ref, h2t_ref, *, alpha):
    hs = hs_ref[...]
    ga = ga_ref[...]
    ya = _head_rms(oa_ref[...], hs) * na_ref[...] * (ga * jax.nn.sigmoid(ga))
    yb = _head_rms(ob_ref[...], hs) * nb_ref[...]
    wa = w_ref.shape[0] // 2
    mix = _dot(ya.astype(BF16), w_ref[0:wa, :]) + _dot(yb.astype(BF16), w_ref[wa:, :])
    x1 = _layernorm_rows(alpha * x_ref[...] + (1.0 + g1_ref[...]) * mix, lg_ref[...], lb_ref[...])
    x1_ref[...] = x1
    h2 = x1 * (1.0 + sc2_ref[...]) + sh2_ref[...]
    h2t_ref[...] = h2.T.astype(BF16)


def _outproj(oa, proj, ob, x, ms, norm_a, norm_b, w_out_bf16, ln_g, ln_b, alpha):
    n, d = x.shape
    tm = ms.tm
    wa = oa.shape[1]
    hs = (np.arange(wa)[:, None] // HEAD_DIM == np.arange(wa)[None, :] // HEAD_DIM).astype(np.float32)
    row = lambda i: (i, 0)
    fixed = lambda i: (0, 0)
    return pl.pallas_call(
        functools.partial(_outproj_kernel, alpha=alpha),
        out_shape=(jax.ShapeDtypeStruct((n, d), F32), jax.ShapeDtypeStruct((d, n), BF16)),
        grid=(n // tm,),
        in_specs=[pl.BlockSpec((tm, wa), row),
                  pl.BlockSpec((None, tm, wa), lambda i: (3, i, 0)),
                  pl.BlockSpec((tm, wa), row),
                  pl.BlockSpec((tm, d), row),
                  ms.spec(2, 1), ms.spec(3, 1), ms.spec(4, 1),
                  pl.BlockSpec((1, wa), fixed), pl.BlockSpec((1, wa), fixed),
                  pl.BlockSpec((wa, wa), fixed),
                  pl.BlockSpec((2 * wa, d), fixed),
                  pl.BlockSpec((1, d), fixed), pl.BlockSpec((1, d), fixed)],
        out_specs=(pl.BlockSpec((tm, d), row), pl.BlockSpec((d, tm), lambda i: (0, i))),
        compiler_params=_params("parallel"),
        name="outproj",
    )(oa, proj, ob, x, ms.arr, ms.arr, ms.arr, norm_a.reshape(1, wa), norm_b.reshape(1, wa),
      jnp.asarray(hs, BF16), w_out_bf16, ln_g.reshape(1, d), ln_b.reshape(1, d))


def _stair_counts():
    return [PEER_TOPK // (a + 1) for a in range(PEER_TOPK)]


def _extract_topk(s, k):
    rows, tn = s.shape
    ridx = lax.broadcasted_iota(jnp.int32, (rows, tn), 0)
    kidx = lax.broadcasted_iota(jnp.int32, (k, tn), 0)

    def body(i, carry):
        cur, vals, rank = carry
        m = jnp.max(cur, axis=0, keepdims=True)
        first = jnp.min(jnp.where(cur == m, ridx, rows), axis=0, keepdims=True)
        hit = ridx == first
        cur = jnp.where(hit, -jnp.inf, cur)
        rank = jnp.where(hit, i.astype(F32), rank)
        vals = jnp.where(kidx == i, m, vals)
        return cur, vals, rank

    init = (s, jnp.zeros((k, tn), F32), jnp.full((rows, tn), float(k), F32))
    cur, vals, rank = lax.fori_loop(0, k, body, init)
    return vals, rank, cur


def _peertopk_kernel(ht_ref, wqt_ref, k1_ref, k2_ref, c1_ref, a1_ref, r2_ref, a2_ref, v1_scr, *, nheads, nkeys):
    tn = ht_ref.shape[1]
    dq = k1_ref.shape[1]
    k = PEER_TOPK
    stair = _stair_counts()
    qt = _dot(wqt_ref[...], ht_ref[...]).astype(BF16)
    k1 = k1_ref[...].astype(BF16)
    k2 = k2_ref[...].astype(BF16)
    b8 = lax.broadcasted_iota(jnp.int32, (8, tn), 0)
    b16 = lax.broadcasted_iota(jnp.int32, (k, tn), 0)
    for h in range(nheads):
        s1 = _dot(k1, qt[(2 * h) * dq:(2 * h + 1) * dq, :])
        s2 = _dot(k2, qt[(2 * h + 1) * dq:(2 * h + 2) * dq, :])
        v1, rank1, _ = _extract_topk(s1, k)
        v2, rank2, _ = _extract_topk(s2, k)
        v1_scr[...] = v1
        pieces = []
        for a in range(k):
            va = jnp.broadcast_to(v1_scr[pl.ds(a, 1), :], (8 if stair[a] <= 8 else k, tn))
            if stair[a] <= 8:
                pieces.append(jnp.where(b8 < stair[a], va + v2[0:8], -jnp.inf))
            else:
                pieces.append(va + v2)
        cand = jnp.concatenate(pieces, axis=0)
        top, _, knocked = _extract_topk(cand, k)
        zsum = jnp.sum(jnp.exp(top - top[0:1]), axis=0, keepdims=True)
        inv_z = 1.0 / zsum
        taken = (knocked == -jnp.inf) & (cand > -jnp.inf)
        c1 = jnp.zeros((nkeys, tn), F32)
        off = 0
        for a in range(k):
            nr = 8 if stair[a] <= 8 else k
            cnt = jnp.sum(taken[off:off + nr].astype(F32), axis=0, keepdims=True)
            c1 = jnp.where(rank1 == float(a), cnt, c1)
            off += nr
        a1 = jnp.where(rank1 < float(k), jnp.exp(s1 - v1[0:1]) * inv_z, 0.0)
        a2 = jnp.where(rank2 < float(k), jnp.exp(s2 - v2[0:1]), 0.0)
        c1_ref[h] = c1
        a1_ref[h] = a1
        r2_ref[h] = rank2.astype(BF16)
        a2_ref[h] = a2.astype(BF16)


def _peer_topk(h2t, wqt_bf16, k1, k2, tn):
    d, n = h2t.shape
    nkeys, dq = k1.shape
    nheads = wqt_bf16.shape[0] // (2 * dq)
    shp = (nheads, nkeys, n)
    blk = pl.BlockSpec((nheads, nkeys, tn), lambda i: (0, 0, i))
    return pl.pallas_call(
        functools.partial(_peertopk_kernel, nheads=nheads, nkeys=nkeys),
        out_shape=(jax.ShapeDtypeStruct(shp, F32), jax.ShapeDtypeStruct(shp, F32),
                   jax.ShapeDtypeStruct(shp, BF16), jax.ShapeDtypeStruct(shp, BF16)),
        grid=(n // tn,),
        in_specs=[pl.BlockSpec((d, tn), lambda i: (0, i)),
                  pl.BlockSpec(wqt_bf16.shape, lambda i: (0, 0)),
                  pl.BlockSpec((nkeys, dq), lambda i: (0, 0)),
                  pl.BlockSpec((nkeys, dq), lambda i: (0, 0))],
        out_specs=(blk, blk, blk, blk),
        scratch_shapes=[pltpu.VMEM((PEER_TOPK, tn), F32)],
        compiler_params=_params("parallel"),
        name="peertopk",
    )(h2t, wqt_bf16, k1, k2)


def _gelu_tanh(x):
    return 0.5 * x * (1.0 + jnp.tanh(0.7978845608028654 * (x + 0.044715 * (x * x * x))))


def _peermain_kernel(ht_ref, u_ref, vt_ref, c1_ref, a1_ref, r2_ref, a2_ref, o_ref, act_ref, g_ref, *, nkeys, nheads):
    j = pl.program_id(1)
    te = u_ref.shape[0]
    tn = ht_ref.shape[1]
    nsub = te // nkeys

    @pl.when(j == 0)
    def _():
        o_ref[...] = jnp.zeros_like(o_ref)

    act_ref[...] = _dot(u_ref[...], ht_ref[...])
    for sub in range(nsub):
        i1 = j * nsub + sub
        w = jnp.zeros((nkeys, tn), BF16)
        for h in range(nheads):
            c1 = jnp.broadcast_to(c1_ref[h, pl.ds(i1, 1), :], (nkeys, tn)).astype(BF16)
            a1 = jnp.broadcast_to(a1_ref[h, pl.ds(i1, 1), :], (nkeys, tn)).astype(BF16)
            w = w + a1 * jnp.where(r2_ref[h] < c1, a2_ref[h], jnp.zeros((), BF16))
        act = act_ref[sub * nkeys:(sub + 1) * nkeys, :]
        g_ref[sub * nkeys:(sub + 1) * nkeys, :] = w * _gelu_tanh(act).astype(BF16)
    o_ref[...] += _dot(vt_ref[...], g_ref[...])


def _peer_main(h2t, u_bf16, vt_bf16, c1, a1, r2, a2, tn, te):
    d, n = h2t.shape
    ne = u_bf16.shape[0]
    nheads, nkeys, _ = c1.shape
    small = pl.BlockSpec((nheads, nkeys, tn), lambda i, j: (0, 0, i))
    return pl.pallas_call(
        functools.partial(_peermain_kernel, nkeys=nkeys, nheads=nheads),
        out_shape=jax.ShapeDtypeStruct((d, n), F32),
        grid=(n // tn, ne // te),
        in_specs=[pl.BlockSpec((d, tn), lambda i, j: (0, i)),
                  pl.BlockSpec((te, d), lambda i, j: (j, 0)),
                  pl.BlockSpec((d, te), lambda i, j: (0, j)),
                  small, small, small, small],
        out_specs=pl.BlockSpec((d, tn), lambda i, j: (0, i)),
        scratch_shapes=[pltpu.VMEM((te, tn), F32), pltpu.VMEM((te, tn), BF16)],
        compiler_params=_params("parallel", "arbitrary"),
        name="peermain",
    )(h2t, u_bf16, vt_bf16, c1, a1, r2, a2)


def _ln2_kernel(x1_ref, fft_ref, g2_ref, lg_ref, lb_ref, y_ref, *, alpha):
    ff = fft_ref[...].T
    y_ref[...] = _layernorm_rows(alpha * x1_ref[...] + (1.0 + g2_ref[...]) * ff, lg_ref[...], lb_ref[...])


def _ln2(x1, fft, col0, ms, ln_g, ln_b, alpha):
    n, d = x1.shape
    tm = ms.tm
    c0 = col0 // tm
    return pl.pallas_call(
        functools.partial(_ln2_kernel, alpha=alpha),
        out_shape=jax.ShapeDtypeStruct((n, d), F32),
        grid=(n // tm,),
        in_specs=[pl.BlockSpec((tm, d), lambda i: (i, 0)),
                  pl.BlockSpec((d, tm), lambda i: (0, c0 + i)),
                  ms.spec(5, 1),
                  pl.BlockSpec((1, d), lambda i: (0, 0)), pl.BlockSpec((1, d), lambda i: (0, 0))],
        out_specs=pl.BlockSpec((tm, d), lambda i: (i, 0)),
        compiler_params=_params("parallel"),
        name="ln2",
    )(x1, fft, ms.arr, ln_g.reshape(1, d), ln_b.reshape(1, d))


def _pick(n, pref):
    for t in pref:
        if n % t == 0:
            return t
    raise ValueError(f"no tile for {n}")


def kernel(x_prompt, x_sample, cache_k, cache_v, state_hgrn, page_table, c_prompt, c_sample, w_ada, b_ada, w_in, lb_raw, norm_a, norm_b, sb_bias, w_out, ln1_g, ln1_b, ln2_g, ln2_b, peer_wq, peer_k1, peer_k2, peer_u, peer_v):
    depth = w_ada.shape[0]
    bp, tp, d = x_prompt.shape
    bs, ts, _ = x_sample.shape
    n_p, n_s = bp * tp, bs * ts
    page, hb = cache_k.shape[2], cache_k.shape[3]
    wb = hb * HEAD_DIM
    alpha = (2.0 * depth) ** 0.25
    lb_all = jnp.cumsum(jax.nn.softmax(lb_raw.astype(F32), axis=0), axis=0)

    tm_p = _pick(tp, (1024, 512, 256, 128))
    tm_s = _pick(n_s, (1024, 512, 256, 128))
    assert n_p % tm_s == 0
    n_all = n_p + n_s
    tn_k = _pick(n_all, (256, 128))
    tn_m = _pick(n_all, (512, 256, 128))
    tq = _pick(tp, (512, 256, 128))
    tk = min(tq, 256)

    yp = x_prompt.reshape(n_p, d)
    ys = x_sample.reshape(n_s, d)
    outs = {k: [] for k in ("kp", "vp", "sp", "ks", "vs", "ss")}
    for l in range(depth):
        mod = _mod(jnp.concatenate([c_prompt, c_sample], axis=0), w_ada[l], b_ada[l])
        ms_p = _ModSpec(mod[:bp].reshape(bp, 1, 6 * d), False, tm_p, tp // tm_p, d)
        ms_s = _ModSpec(jnp.repeat(mod[bp:], ts, axis=0), True, tm_s, 1, d)
        w_in_b = w_in[l].astype(BF16)
        w_out_b = w_out[l].astype(BF16)
        lb = lb_all[l]

        proj_p = _inproj(yp, ms_p, w_in_b, wb)
        proj_s = _inproj(ys, ms_s, w_in_b, wb)

        oa_p, sp = _hgrn(proj_p, lb, None, bp, tp)
        oa_s, ss = _hgrn(proj_s, lb, state_hgrn[l], bs, ts)

        ob_p = _sb_prompt(proj_p, sb_bias[l].astype(F32), bp, tp, tq, tk)
        ck = cache_k[l].reshape(cache_k.shape[1], page * hb, HEAD_DIM)
        cv = cache_v[l].reshape(cache_v.shape[1], page * hb, HEAD_DIM)
        heads_first = lambda a: a.reshape(bs, ts, hb, HEAD_DIM).transpose(0, 2, 1, 3)
        ob_s = _sb_paged(heads_first(proj_s[4]).reshape(bs, hb * ts, HEAD_DIM), heads_first(proj_s[5]),
                         heads_first(proj_s[6]), ck, cv, page_table, sb_bias[l], page)
        ob_s = ob_s.reshape(bs, hb, ts, HEAD_DIM).transpose(0, 2, 1, 3).reshape(n_s, wb)

        x1_p, h2t_p = _outproj(oa_p, proj_p, ob_p, yp, ms_p, norm_a[l], norm_b[l], w_out_b, ln1_g[l], ln1_b[l], alpha)
        x1_s, h2t_s = _outproj(oa_s, proj_s, ob_s, ys, ms_s, norm_a[l], norm_b[l], w_out_b, ln1_g[l], ln1_b[l], alpha)

        h2t = jnp.concatenate([h2t_p, h2t_s], axis=1)
        c1, a1, r2, a2 = _peer_topk(h2t, peer_wq[l].T.astype(BF16), peer_k1[l], peer_k2[l], tn_k)
        fft = _peer_main(h2t, peer_u[l].astype(BF16), peer_v[l].T.astype(BF16), c1, a1, r2, a2,
                         tn_m, 8 * peer_k1.shape[1])

        yp = _ln2(x1_p, fft, 0, ms_p, ln2_g[l], ln2_b[l], alpha)
        ys = _ln2(x1_s, fft, n_p, ms_s, ln2_g[l], ln2_b[l], alpha)

        outs["kp"].append(proj_p[5].reshape(bp, tp, hb, HEAD_DIM))
        outs["vp"].append(proj_p[6].reshape(bp, tp, hb, HEAD_DIM))
        outs["sp"].append(sp)
        outs["ks"].append(proj_s[5].reshape(bs, ts, hb, HEAD_DIM))
        outs["vs"].append(proj_s[6].reshape(bs, ts, hb, HEAD_DIM))
        outs["ss"].append(ss)
    return (yp.reshape(bp, tp, d), ys.reshape(bs, ts, d),
            jnp.stack(outs["kp"]), jnp.stack(outs["vp"]), jnp.stack(outs["sp"]),
            jnp.stack(outs["ks"]), jnp.stack(outs["vs"]), jnp.stack(outs["ss"]))
```

```python
import functools
import math

import numpy as np
import jax
import jax.numpy as jnp
from jax import lax
from jax.experimental import pallas as pl
from jax.experimental.pallas import tpu as pltpu

F32 = jnp.float32
BF16 = jnp.bfloat16

HEAD_DIM = 64
LANES = 128
LN_EPS = 1e-5
RMS_EPS = 1e-6
PEER_TOPK = 16
VMEM_LIMIT = 56 * 1024 * 1024

_NT = (((1,), (1,)), ((), ()))
_TN = (((0,), (0,)), ((), ()))


def _dot(a, b):
    return jnp.dot(a, b, preferred_element_type=F32)


def _dot_nt(a, b):
    return lax.dot_general(a, b, _NT, preferred_element_type=F32)


def _split3(x):
    hi = x.astype(BF16)
    r1 = x - hi.astype(F32)
    mid = r1.astype(BF16)
    lo = (r1 - mid.astype(F32)).astype(BF16)
    return hi, mid, lo


def _params(*sem):
    return pltpu.CompilerParams(dimension_semantics=sem, vmem_limit_bytes=VMEM_LIMIT)


def _mod_kernel(c_ref, w_ref, b_ref, o_ref):
    o_ref[...] = _dot(c_ref[...].astype(BF16), w_ref[...].astype(BF16)) + b_ref[...]


def _mod(c, w, b):
    n, d = c.shape
    width = w.shape[1]
    tn = 1024
    return pl.pallas_call(
        _mod_kernel,
        out_shape=jax.ShapeDtypeStruct((n, width), F32),
        grid=(width // tn,),
        in_specs=[pl.BlockSpec((n, d), lambda j: (0, 0)),
                  pl.BlockSpec((d, tn), lambda j: (0, j)),
                  pl.BlockSpec((1, tn), lambda j: (0, j))],
        out_specs=pl.BlockSpec((n, tn), lambda j: (0, j)),
        compiler_params=_params("parallel"),
        name="mod",
    )(c, w, b.reshape(1, width))


class _ModSpec:
    def __init__(self, arr, per_token, tm, tiles_per_seq, d):
        self.arr, self.per_token, self.tm, self.tps, self.d = arr, per_token, tm, tiles_per_seq, d

    def spec(self, part, ngrid):
        d, tps = self.d, self.tps
        if self.per_token:
            if ngrid == 1:
                return pl.BlockSpec((self.tm, d), lambda i: (i, part))
            return pl.BlockSpec((self.tm, d), lambda i, j: (i, part))
        if ngrid == 1:
            return pl.BlockSpec((None, 1, d), lambda i: (i // tps, 0, part))
        return pl.BlockSpec((None, 1, d), lambda i, j: (i // tps, 0, part))


def _inproj_kernel(x_ref, sh_ref, sc_ref, w_ref, o_ref, h_ref):
    @pl.when(pl.program_id(1) == 0)
    def _():
        h_ref[...] = (x_ref[...] * (1.0 + sc_ref[...]) + sh_ref[...]).astype(BF16)

    o_ref[...] = _dot(h_ref[...], w_ref[...])


def _inproj(x, ms, w_bf16, tn):
    n, d = x.shape
    tm = ms.tm
    nparts = w_bf16.shape[1] // tn
    return pl.pallas_call(
        _inproj_kernel,
        out_shape=jax.ShapeDtypeStruct((nparts, n, tn), F32),
        grid=(n // tm, nparts),
        in_specs=[pl.BlockSpec((tm, d), lambda i, j: (i, 0)),
                  ms.spec(0, 2), ms.spec(1, 2),
                  pl.BlockSpec((d, tn), lambda i, j: (0, j))],
        out_specs=pl.BlockSpec((None, tm, tn), lambda i, j: (j, i, 0)),
        scratch_shapes=[pltpu.VMEM((tm, d), BF16)],
        compiler_params=_params("parallel", "arbitrary"),
        name="inproj",
    )(x, ms.arr, ms.arr, w_bf16)


def _hgrn_consts(c, lseq):
    nlev = int(math.log2(lseq))
    r = np.arange(c)
    seg, pos = r // lseq, r % lseq
    j = np.arange(c)
    same = seg[:, None] == seg[None, :]
    mats = [same & (j[None, :] <= r[:, None]),
            same & (j[None, :] > r[:, None])]
    rq = np.zeros((nlev, c, 1), np.float32)
    mb = np.zeros((nlev + 1, c, c), np.float32)
    for lev in range(nlev):
        half, blk = 1 << lev, 2 << lev
        is_q = (pos % blk) >= half
        ref = (r // blk) * blk + half - 1
        m = np.where(is_q[:, None],
                     (j[None, :] > ref[:, None]) & (j[None, :] <= r[:, None]),
                     (j[None, :] > r[:, None]) & (j[None, :] <= ref[:, None]))
        mats.append(m)
        rq[lev, :, 0] = is_q
        mb[lev] = (r[:, None] // blk) == (r[None, :] // blk)
    mb[nlev] = np.eye(c)
    a = np.concatenate(mats, axis=0).astype(np.float32)
    rq = np.broadcast_to(rq, (nlev, c, LANES)).copy()
    mb2 = np.concatenate([mb, mb], axis=1)
    return jnp.asarray(a, BF16), jnp.asarray(rq), jnp.asarray(mb2), nlev


def _hgrn_kernel(*refs, c, lseq, nch, nlev, has_s0):
    if has_s0:
        q_ref, f_ref, i_ref, lb_ref, a_ref, rq_ref, mb_ref, s0_ref, o_ref, s_ref, st_ref = refs
    else:
        q_ref, f_ref, i_ref, lb_ref, a_ref, rq_ref, mb_ref, o_ref, s_ref, st_ref = refs
    nsq = c // lseq
    lbv = lb_ref[...]
    lane_lo = lax.broadcasted_iota(jnp.int32, (c, LANES), 1) < HEAD_DIM
    r128 = lax.broadcasted_iota(jnp.int32, (LANES, nsq * LANES), 0)
    c128 = lax.broadcasted_iota(jnp.int32, (LANES, nsq * LANES), 1) % LANES
    same_head = (r128 < HEAD_DIM) == (c128 < HEAD_DIM)
    row_seq = lax.broadcasted_iota(jnp.int32, (c, LANES), 0) // lseq
    zero64 = jnp.zeros((HEAD_DIM, HEAD_DIM), F32)

    if has_s0:
        for s in range(nsq):
            top = jnp.concatenate([s0_ref[s, 0], zero64], axis=1)
            bot = jnp.concatenate([zero64, s0_ref[s, 1]], axis=1)
            st_ref[:, s * LANES:(s + 1) * LANES] = jnp.concatenate([top, bot], axis=0).T
    else:
        st_ref[...] = jnp.zeros_like(st_ref)

    def expand(x):
        if nsq == 1:
            return x
        return jnp.concatenate([jnp.where(row_seq == s, x, 0.0) for s in range(nsq)], axis=1)

    def heads_on_rows(x):
        return jnp.concatenate([jnp.where(lane_lo, x, 0.0), jnp.where(lane_lo, 0.0, x)], axis=0)

    def chunk(ci, carry):
        r0 = pl.multiple_of(ci * c, c)
        qa = q_ref[pl.ds(r0, c), :]
        z = f_ref[pl.ds(r0, c), :]
        v = i_ref[pl.ds(r0, c), :]
        lf = jnp.log(lbv + (1.0 - lbv) * jax.nn.sigmoid(z))
        kin = (1.0 - lbv) * jax.nn.sigmoid(-z)
        qs = qa * jax.nn.sigmoid(qa)
        hi, mid, lo = _split3(lf)
        a = a_ref[...]
        g = _dot(a, hi) + _dot(a, mid) + _dot(a, lo)
        eb = jnp.exp(g[0:c])
        qg = qs * eb
        kd = kin * jnp.exp(g[c:2 * c])
        st = st_ref[...]
        o = _dot_nt(expand(qg).astype(BF16), st.astype(BF16))

        kin_b = kin.astype(BF16)
        sc = mb_ref[nlev] * _dot_nt(heads_on_rows(qs).astype(BF16), kin_b)
        for lev in range(nlev):
            y = jnp.exp(g[(2 + lev) * c:(3 + lev) * c])
            isq = rq_ref[lev]
            ql = qs * (y * isq)
            kl = kin * (y * (1.0 - isq))
            sc = sc + mb_ref[lev] * _dot_nt(heads_on_rows(ql).astype(BF16), kl.astype(BF16))
        r = _dot(sc.astype(BF16), v.astype(BF16))
        o_ref[pl.ds(r0, c), :] = o + jnp.where(lane_lo, r[0:c], r[c:2 * c])

        if nsq == 1:
            dl = eb[c - 1:c, :]
        else:
            dl = jnp.concatenate([eb[(s + 1) * lseq - 1:(s + 1) * lseq, :] for s in range(nsq)], axis=1)
        u = _dot(v.T.astype(BF16), expand(kd).astype(BF16))
        st_ref[...] = st * dl + jnp.where(same_head, u, 0.0)
        return carry

    lax.fori_loop(0, nch, chunk, 0)

    for s in range(nsq):
        sbd = st_ref[:, s * LANES:(s + 1) * LANES].T
        s_ref[s, 0] = sbd[0:HEAD_DIM, 0:HEAD_DIM]
        s_ref[s, 1] = sbd[HEAD_DIM:, HEAD_DIM:]


def _hgrn(proj, lb, s0, nseq, t):
    _, n, width = proj.shape
    npair = width // LANES
    c = LANES
    lseq = min(t, c)
    nsq = c // lseq
    if t >= c:
        nch, rows, ngroups = t // c, t, nseq
    else:
        nch, rows, ngroups = 1, c, nseq // nsq
    a, rq, mb, nlev = _hgrn_consts(c, lseq)
    has_s0 = s0 is not None
    ins = [proj, proj, proj, lb.reshape(1, width), a, rq, mb]
    in_specs = [pl.BlockSpec((None, rows, LANES), lambda g, p: (0, g, p)),
                pl.BlockSpec((None, rows, LANES), lambda g, p: (1, g, p)),
                pl.BlockSpec((None, rows, LANES), lambda g, p: (2, g, p)),
                pl.BlockSpec((1, LANES), lambda g, p: (0, p)),
                pl.BlockSpec(a.shape, lambda g, p: (0, 0)),
                pl.BlockSpec(rq.shape, lambda g, p: (0, 0, 0)),
                pl.BlockSpec(mb.shape, lambda g, p: (0, 0, 0))]
    if has_s0:
        ins.append(s0)
        in_specs.append(pl.BlockSpec((nsq, 2, HEAD_DIM, HEAD_DIM), lambda g, p: (g, p, 0, 0)))
    kern = functools.partial(_hgrn_kernel, c=c, lseq=lseq, nch=nch, nlev=nlev, has_s0=has_s0)
    return pl.pallas_call(
        kern,
        out_shape=(jax.ShapeDtypeStruct((n, width), F32),
                   jax.ShapeDtypeStruct((nseq, 2 * npair, HEAD_DIM, HEAD_DIM), F32)),
        grid=(ngroups, npair),
        in_specs=in_specs,
        out_specs=(pl.BlockSpec((rows, LANES), lambda g, p: (g, p)),
                   pl.BlockSpec((nsq, 2, HEAD_DIM, HEAD_DIM), lambda g, p: (g, p, 0, 0))),
        scratch_shapes=[pltpu.VMEM((LANES, nsq * LANES), F32)],
        compiler_params=_params("parallel", "parallel"),
        name="hgrn",
    )(*ins)


LOG2E = 1.4426950408889634


def _sb_block(z2, tail, uneg, mask):
    sp = jnp.maximum(z2, 0.0) + jnp.log2(1.0 + jnp.exp2(-jnp.abs(z2)))
    if mask is not None:
        sp = jnp.where(mask, sp, 0.0)
    rr = _dot(sp.astype(BF16), uneg)
    a = jnp.exp2(z2 - sp + (tail + rr))
    if mask is not None:
        a = jnp.where(mask, a, 0.0)
    return a, tail - jnp.sum(sp, axis=1, keepdims=True)


def _uneg_const(tk):
    j = np.arange(tk)
    return jnp.asarray(-(j[:, None] > j[None, :]).astype(np.float32), BF16)


def _sb_kernel(q_ref, k_ref, v_ref, bias_ref, u2_ref, o_ref, *, tq, tk):
    qi = pl.program_id(2)
    u2 = u2_ref[...]
    scale = HEAD_DIM ** -0.5 * LOG2E
    lane_lo = lax.broadcasted_iota(jnp.int32, (tq, LANES), 1) < HEAD_DIM
    q = q_ref[...] * scale
    rows = qi * tq + lax.broadcasted_iota(jnp.int32, (tq, tk), 0)
    cols = lax.broadcasted_iota(jnp.int32, (tq, tk), 1)
    ndiag = tq // tk
    qh = (jnp.where(lane_lo, q, 0.0).astype(BF16), jnp.where(lane_lo, 0.0, q).astype(BF16))
    bias = (bias_ref[2 * pl.program_id(1)] * LOG2E, bias_ref[2 * pl.program_id(1) + 1] * LOG2E)

    def block(kb, carry, masked):
        k0 = pl.multiple_of(kb * tk, tk)
        kk = k_ref[pl.ds(k0, tk), :].astype(BF16)
        vv = v_ref[pl.ds(k0, tk), :].astype(BF16)
        mask = (cols + kb * tk < rows) if masked else None
        out = []
        for h in range(2):
            acc, tail = carry[h]
            a, tail = _sb_block(_dot_nt(qh[h], kk) + bias[h], tail, u2, mask)
            out.append((acc + _dot(a.astype(BF16), vv), tail))
        return tuple(out)

    zero = (jnp.zeros((tq, LANES), F32), jnp.zeros((tq, 1), F32))
    carry = (zero, zero)
    for d in range(ndiag):
        carry = block(qi * ndiag + (ndiag - 1 - d), carry, True)
    carry = lax.fori_loop(0, qi * ndiag, lambda i, c: block(qi * ndiag - 1 - i, c, False), carry)
    o_ref[...] = jnp.where(lane_lo, carry[0][0], carry[1][0])


def _sb_prompt(proj, bias, nseq, t, tq, tk):
    _, n, width = proj.shape
    npair = width // LANES
    nq = t // tq
    u2 = _uneg_const(tk)
    kern = functools.partial(_sb_kernel, tq=tq, tk=tk)
    return pl.pallas_call(
        kern,
        out_shape=jax.ShapeDtypeStruct((n, width), F32),
        grid=(nseq, npair, nq),
        in_specs=[pl.BlockSpec((None, tq, LANES), lambda b, p, i: (4, b * nq + i, p)),
                  pl.BlockSpec((None, t, LANES), lambda b, p, i: (5, b, p)),
                  pl.BlockSpec((None, t, LANES), lambda b, p, i: (6, b, p)),
                  pl.BlockSpec(memory_space=pltpu.SMEM),
                  pl.BlockSpec(u2.shape, lambda b, p, i: (0, 0))],
        out_specs=pl.BlockSpec((tq, LANES), lambda b, p, i: (b * nq + i, p)),
        compiler_params=_params("parallel", "parallel", "arbitrary"),
        name="sb",
    )(proj, proj, proj, bias, u2)


def _sbpaged_kernel(pt_ref, q_ref, kn_ref, vn_ref, bias_ref, u2_ref, sel_ref, selt_ref, *rest, npages, page, t, nh):
    k_refs = rest[:npages]
    v_refs = rest[npages:2 * npages]
    o_ref = rest[2 * npages]
    rows = nh * t
    wide = page * nh
    scale = HEAD_DIM ** -0.5 * LOG2E
    u2 = u2_ref[...]
    sel = sel_ref[...]
    selt = selt_ref[...]
    qa = (q_ref[...] * scale).astype(BF16)
    bias = bias_ref[...]
    same_head = (lax.broadcasted_iota(jnp.int32, (rows, wide), 0) // t
                 == lax.broadcasted_iota(jnp.int32, (rows, wide), 1) % nh)

    pad = jnp.zeros((wide - t * nh, HEAD_DIM), F32)
    order = list(range(npages - 1, -1, -1))
    xk = [jnp.concatenate([kn_ref[...], pad], axis=0)] + [k_refs[p][...].reshape(wide, HEAD_DIM) for p in order]
    xv = [jnp.concatenate([vn_ref[...], pad], axis=0)] + [v_refs[p][...].reshape(wide, HEAD_DIM) for p in order]
    nblk = npages + 1
    prod = jnp.concatenate([jnp.where(same_head, _dot_nt(qa, x.astype(BF16)), 0.0) for x in xk], axis=0)
    hi = prod.astype(BF16)
    lo = (prod - hi.astype(F32)).astype(BF16)
    z2 = _dot(hi, sel) + _dot(lo, sel) + jnp.concatenate([bias] * nblk, axis=0)
    qpos = lax.broadcasted_iota(jnp.int32, (rows, page), 0) % t
    kpos = lax.broadcasted_iota(jnp.int32, (rows, page), 1)
    visible = jnp.concatenate([kpos < qpos] + [jnp.ones((rows, page), jnp.bool_)] * npages, axis=0)
    sp = jnp.where(visible, jnp.maximum(z2, 0.0) + jnp.log2(1.0 + jnp.exp2(-jnp.abs(z2))), 0.0)
    rr = _dot(sp.astype(BF16), u2)
    blk_sum = jnp.sum(sp, axis=1, keepdims=True)
    tails = [jnp.zeros((rows, 1), F32)]
    for b in range(nblk - 1):
        tails.append(tails[-1] - blk_sum[b * rows:(b + 1) * rows])
    a = jnp.where(visible, jnp.exp2(z2 - sp + (jnp.concatenate(tails, axis=0) + rr)), 0.0)
    spread = _dot(a.astype(BF16), selt)
    acc = jnp.zeros((rows, HEAD_DIM), F32)
    for b in range(nblk):
        ab = jnp.where(same_head, spread[b * rows:(b + 1) * rows], 0.0).astype(BF16)
        acc = acc + _dot(ab, xv[b].astype(BF16))
    o_ref[...] = acc


def _sb_paged(q, kn, vn, cache_k, cache_v, layer, page_table, sb_bias):
    nseq, rows, _ = q.shape
    page, nh = cache_k.shape[2], cache_k.shape[3]
    t = rows // nh
    npages = page_table.shape[1]
    u2 = _uneg_const(page)
    sel_np = (np.arange(page * nh)[:, None] // nh == np.arange(page)[None, :]).astype(np.float32)
    sel, selt = jnp.asarray(sel_np, BF16), jnp.asarray(sel_np.T.copy(), BF16)
    bias = jnp.broadcast_to(jnp.repeat(sb_bias.astype(F32) * LOG2E, t)[:, None], (rows, page))
    kern = functools.partial(_sbpaged_kernel, npages=npages, page=page, t=t, nh=nh)

    def page_spec(p):
        return pl.BlockSpec((None, None, page, nh, HEAD_DIM), lambda b, pt: (layer, pt[b, p], 0, 0, 0))

    grid_spec = pltpu.PrefetchScalarGridSpec(
        num_scalar_prefetch=1,
        grid=(nseq,),
        in_specs=[pl.BlockSpec((None, rows, HEAD_DIM), lambda b, pt: (b, 0, 0)),
                  pl.BlockSpec((None, rows, HEAD_DIM), lambda b, pt: (b, 0, 0)),
                  pl.BlockSpec((None, rows, HEAD_DIM), lambda b, pt: (b, 0, 0)),
                  pl.BlockSpec((rows, page), lambda b, pt: (0, 0)),
                  pl.BlockSpec(u2.shape, lambda b, pt: (0, 0)),
                  pl.BlockSpec(sel.shape, lambda b, pt: (0, 0)),
                  pl.BlockSpec(selt.shape, lambda b, pt: (0, 0))]
                 + [page_spec(p) for p in range(npages)] * 2,
        out_specs=pl.BlockSpec((None, rows, HEAD_DIM), lambda b, pt: (b, 0, 0)),
    )
    return pl.pallas_call(
        kern,
        out_shape=jax.ShapeDtypeStruct((nseq, rows, HEAD_DIM), F32),
        grid_spec=grid_spec,
        compiler_params=_params("arbitrary"),
        name="sbpaged",
    )(page_table, q, kn, vn, bias, u2, sel, selt, *([cache_k] * npages), *([cache_v] * npages))


def _layernorm_rows(x, g, b):
    mu = jnp.mean(x, axis=-1, keepdims=True)
    xc = x - mu
    var = jnp.mean(xc * xc, axis=-1, keepdims=True)
    return xc * lax.rsqrt(var + LN_EPS) * g + b


def _head_rms(o, hsum):
    sq = o * o
    hi = sq.astype(BF16)
    lo = (sq - hi.astype(F32)).astype(BF16)
    ms = (_dot(hi, hsum) + _dot(lo, hsum)) * (1.0 / HEAD_DIM)
    return o * lax.rsqrt(ms + RMS_EPS)


def _outproj_kernel(oa_ref, ga_ref, ob_ref, x_ref, g1_ref, sh2_ref, sc2_ref, na_ref, nb_ref, hs_ref,
                    w_ref, lg_ref, lb_ref, x1_ref, h2t_ref, *, alpha):
    hs = hs_ref[...]
    ga = ga_ref[...]
    ya = _head_rms(oa_ref[...], hs) * na_ref[...] * (ga * jax.nn.sigmoid(ga))
    yb = _head_rms(ob_ref[...], hs) * nb_ref[...]
    wa = w_ref.shape[0] // 2
    mix = _dot(ya.astype(BF16), w_ref[0:wa, :]) + _dot(yb.astype(BF16), w_ref[wa:, :])
    x1 = _layernorm_rows(alpha * x_ref[...] + (1.0 + g1_ref[...]) * mix, lg_ref[...], lb_ref[...])
    x1_ref[...] = x1
    h2 = x1 * (1.0 + sc2_ref[...]) + sh2_ref[...]
    h2t_ref[...] = h2.T.astype(BF16)


def _outproj(oa, proj, ob, x, ms, norm_a, norm_b, w_out_bf16, ln_g, ln_b, alpha):
    n, d = x.shape
    tm = ms.tm
    wa = oa.shape[1]
    hs = (np.arange(wa)[:, None] // HEAD_DIM == np.arange(wa)[None, :] // HEAD_DIM).astype(np.float32)
    row = lambda i: (i, 0)
    fixed = lambda i: (0, 0)
    return pl.pallas_call(
        functools.partial(_outproj_kernel, alpha=alpha),
        out_shape=(jax.ShapeDtypeStruct((n, d), F32), jax.ShapeDtypeStruct((d, n), BF16)),
        grid=(n // tm,),
        in_specs=[pl.BlockSpec((tm, wa), row),
                  pl.BlockSpec((None, tm, wa), lambda i: (3, i, 0)),
                  pl.BlockSpec((tm, wa), row),
                  pl.BlockSpec((tm, d), row),
                  ms.spec(2, 1), ms.spec(3, 1), ms.spec(4, 1),
                  pl.BlockSpec((1, wa), fixed), pl.BlockSpec((1, wa), fixed),
                  pl.BlockSpec((wa, wa), fixed),
                  pl.BlockSpec((2 * wa, d), fixed),
                  pl.BlockSpec((1, d), fixed), pl.BlockSpec((1, d), fixed)],
        out_specs=(pl.BlockSpec((tm, d), row), pl.BlockSpec((d, tm), lambda i: (0, i))),
        compiler_params=_params("parallel"),
        name="outproj",
    )(oa, proj, ob, x, ms.arr, ms.arr, ms.arr, norm_a.reshape(1, wa), norm_b.reshape(1, wa),
      jnp.asarray(hs, BF16), w_out_bf16, ln_g.reshape(1, d), ln_b.reshape(1, d))


MARK0 = -(2.0 ** 127)
MSTEP = 2.0 ** 105
CAND_ROWS = 64


def _stair_cells():
    k = PEER_TOPK
    return [(a, b) for a in range(k) for b in range(k) if (a + 1) * (b + 1) <= k]


def _cand_consts():
    cells = _stair_cells()
    g1 = np.zeros((CAND_ROWS, PEER_TOPK), np.float32)
    g2 = np.zeros((CAND_ROWS, PEER_TOPK), np.float32)
    for r, (a, b) in enumerate(cells):
        g1[r, a] = 1.0
        g2[r, b] = 1.0
    return jnp.asarray(g1, BF16), jnp.asarray(g2, BF16), jnp.asarray(g1.T.copy(), BF16), len(cells)


def _extract_topk(s, k):
    rows, tn = s.shape
    ridx = lax.broadcasted_iota(jnp.int32, (rows, tn), 0)
    kidx = lax.broadcasted_iota(jnp.int32, (k, tn), 0)

    def body(i, carry):
        cur, vals, rank = carry
        m = jnp.max(cur, axis=0, keepdims=True)
        first = jnp.min(jnp.where(cur == m, ridx, rows), axis=0, keepdims=True)
        hit = ridx == first
        cur = jnp.where(hit, -jnp.inf, cur)
        rank = jnp.where(hit, jnp.full((1, tn), i, jnp.int32).astype(F32), rank)
        vals = jnp.where(kidx == i, m, vals)
        return cur, vals, rank

    init = (s, jnp.zeros((k, tn), F32), jnp.full((rows, tn), float(k), F32))
    _, vals, rank = lax.fori_loop(0, k, body, init)
    return vals, rank


def _knock_topk(cur_ref, val_ref, narr, k):
    tn = cur_ref.shape[2]

    def body(i, carry):
        mark = MARK0 - jnp.full((1, tn), i, jnp.int32).astype(F32) * MSTEP
        for a in range(narr):
            cur = cur_ref[a]
            m = jnp.max(cur, axis=0, keepdims=True)
            cur_ref[a] = jnp.where(cur == m, mark, cur)
            val_ref[a, pl.ds(i, 1), :] = m
        return carry

    lax.fori_loop(0, k, body, 0)


def _tie_free(cur_ref, narr, k, valid=None):
    ok = None
    for a in range(narr):
        gone = cur_ref[a] <= MARK0
        if valid is not None:
            gone = gone & valid
        good = jnp.sum(gone.astype(F32), axis=0, keepdims=True) == float(k)
        ok = good if ok is None else (ok & good)
    return jnp.min(ok.astype(jnp.int32)) == 1


def _exact_topk_into(src_ref, cur_ref, val_ref, narr, k):
    for a in range(narr):
        s = src_ref[a]
        vals, rank = _extract_topk(s, k)
        val_ref[a] = vals
        cur_ref[a] = jnp.where(rank < float(k), MARK0 - rank * MSTEP, s)


def _bcast_rows_bf16(row, nrows):
    tile = jnp.broadcast_to(row, (16, row.shape[1])).astype(BF16)
    return jnp.concatenate([tile] * (nrows // 16), axis=0)


def _peertopk_kernel(ht_ref, wqt_ref, k1_ref, k2_ref, g1_ref, g2_ref, ga_ref, c1_ref, a1_ref, r2_ref, a2_ref,
                     s_scr, cur_scr, val_scr, cand_scr, ccur_scr, cval_scr, *, nheads, nkeys, ncells):
    tn = ht_ref.shape[1]
    dq = k1_ref.shape[1]
    k = PEER_TOPK
    qt = _dot(wqt_ref[...], ht_ref[...]).astype(BF16)
    k1 = k1_ref[...].astype(BF16)
    k2 = k2_ref[...].astype(BF16)
    for h in range(nheads):
        s_scr[2 * h] = _dot(k1, qt[(2 * h) * dq:(2 * h + 1) * dq, :])
        s_scr[2 * h + 1] = _dot(k2, qt[(2 * h + 1) * dq:(2 * h + 2) * dq, :])
    cur_scr[...] = s_scr[...]
    _knock_topk(cur_scr, val_scr, 2 * nheads, k)

    @pl.when(jnp.logical_not(_tie_free(cur_scr, 2 * nheads, k)))
    def _():
        _exact_topk_into(s_scr, cur_scr, val_scr, 2 * nheads, k)

    g1, g2 = g1_ref[...], g2_ref[...]
    valid = lax.broadcasted_iota(jnp.int32, (CAND_ROWS, tn), 0) < ncells

    def pick(g, v):
        hi, mid, lo = _split3(v)
        return _dot(g, hi) + _dot(g, mid) + _dot(g, lo)

    for h in range(nheads):
        cand = jnp.where(valid, pick(g1, val_scr[2 * h]) + pick(g2, val_scr[2 * h + 1]), -jnp.inf)
        cand_scr[h] = cand
        ccur_scr[h] = cand
    _knock_topk(ccur_scr, cval_scr, nheads, k)

    @pl.when(jnp.logical_not(_tie_free(ccur_scr, nheads, k, valid)))
    def _():
        _exact_topk_into(cand_scr, ccur_scr, cval_scr, nheads, k)

    ga = ga_ref[...]
    inv_step = 1.0 / MSTEP
    for h in range(nheads):
        taken = ((ccur_scr[h] <= MARK0) & valid).astype(BF16)
        cnt = _dot(ga, taken)
        top = cval_scr[h]
        inv_z = 1.0 / jnp.sum(jnp.exp(top - top[0:1]), axis=0, keepdims=True)
        s1, s2 = s_scr[2 * h], s_scr[2 * h + 1]
        cur1, cur2 = cur_scr[2 * h], cur_scr[2 * h + 1]
        sel1, sel2 = cur1 <= MARK0, cur2 <= MARK0
        rank1 = jnp.where(sel1, (MARK0 - cur1) * inv_step, float(k)).astype(BF16)
        rank2 = jnp.where(sel2, (MARK0 - cur2) * inv_step, float(k))
        c1 = jnp.zeros((nkeys, tn), BF16)
        for a in range(k):
            c1 = jnp.where(rank1 == a, _bcast_rows_bf16(cnt[a:a + 1], nkeys), c1)
        v1 = val_scr[2 * h]
        v2 = val_scr[2 * h + 1]
        c1_ref[h] = c1.astype(F32)
        a1_ref[h] = jnp.where(sel1, jnp.exp(s1 - v1[0:1]) * inv_z, 0.0)
        r2_ref[h] = rank2.astype(BF16)
        a2_ref[h] = jnp.where(sel2, jnp.exp(s2 - v2[0:1]), 0.0).astype(BF16)


def _peer_topk(h2t, wqt_bf16, k1, k2, tn):
    d, n = h2t.shape
    nkeys, dq = k1.shape
    nheads = wqt_bf16.shape[0] // (2 * dq)
    g1, g2, ga, ncells = _cand_consts()
    shp = (nheads, nkeys, n)
    blk = pl.BlockSpec((nheads, nkeys, tn), lambda i: (0, 0, i))
    fixed = lambda i: (0, 0)
    k = PEER_TOPK
    return pl.pallas_call(
        functools.partial(_peertopk_kernel, nheads=nheads, nkeys=nkeys, ncells=ncells),
        out_shape=(jax.ShapeDtypeStruct(shp, F32), jax.ShapeDtypeStruct(shp, F32),
                   jax.ShapeDtypeStruct(shp, BF16), jax.ShapeDtypeStruct(shp, BF16)),
        grid=(n // tn,),
        in_specs=[pl.BlockSpec((d, tn), lambda i: (0, i)),
                  pl.BlockSpec(wqt_bf16.shape, fixed),
                  pl.BlockSpec((nkeys, dq), fixed), pl.BlockSpec((nkeys, dq), fixed),
                  pl.BlockSpec(g1.shape, fixed), pl.BlockSpec(g2.shape, fixed), pl.BlockSpec(ga.shape, fixed)],
        out_specs=(blk, blk, blk, blk),
        scratch_shapes=[pltpu.VMEM((2 * nheads, nkeys, tn), F32), pltpu.VMEM((2 * nheads, nkeys, tn), F32),
                        pltpu.VMEM((2 * nheads, k, tn), F32),
                        pltpu.VMEM((nheads, CAND_ROWS, tn), F32), pltpu.VMEM((nheads, CAND_ROWS, tn), F32),
                        pltpu.VMEM((nheads, k, tn), F32)],
        compiler_params=_params("parallel"),
        name="peertopk",
    )(h2t, wqt_bf16, k1, k2, g1, g2, ga)


GATE_LANES = 256


def _gelu_tanh(x):
    return 0.5 * x * (1.0 + jnp.tanh(0.7978845608028654 * (x + 0.044715 * (x * x * x))))


def _peermain_kernel(ht_ref, u_ref, vt_ref, c1_ref, a1_ref, r2_ref, a2_ref, o_ref, act_ref, g_ref, *, nkeys, nheads):
    j = pl.program_id(1)
    te, tn = act_ref.shape
    gl = min(GATE_LANES, tn)
    assert tn % gl == 0
    zero = jnp.zeros((), BF16)

    @pl.when(j == 0)
    def _():
        o_ref[...] = jnp.zeros_like(o_ref)

    act_ref[...] = _dot(u_ref[...], ht_ref[...])
    for sub in range(te // nkeys):
        arow = slice(sub * nkeys, (sub + 1) * nkeys)
        for lt in range(tn // gl):
            cols = slice(lt * gl, (lt + 1) * gl)
            w = jnp.zeros((nkeys, gl), BF16)
            for h in range(nheads):
                c1 = _bcast_rows_bf16(c1_ref[h, sub:sub + 1, cols], nkeys)
                a1 = _bcast_rows_bf16(a1_ref[h, sub:sub + 1, cols], nkeys)
                w = w + a1 * jnp.where(r2_ref[h, :, cols] < c1, a2_ref[h, :, cols], zero)
            g_ref[arow, cols] = w * _gelu_tanh(act_ref[arow, cols]).astype(BF16)
    o_ref[...] += _dot(vt_ref[...], g_ref[...])


def _peer_main(h2t, u_bf16, vt_bf16, c1, a1, r2, a2, tn, te):
    d, n = h2t.shape
    ne = u_bf16.shape[0]
    nheads, nkeys, _ = c1.shape
    small = pl.BlockSpec((nheads, nkeys, tn), lambda i, j: (0, 0, i))
    per_tile = pl.BlockSpec((nheads, te // nkeys, tn), lambda i, j: (0, j, i))
    return pl.pallas_call(
        functools.partial(_peermain_kernel, nkeys=nkeys, nheads=nheads),
        out_shape=jax.ShapeDtypeStruct((d, n), F32),
        grid=(n // tn, ne // te),
        in_specs=[pl.BlockSpec((d, tn), lambda i, j: (0, i)),
                  pl.BlockSpec((te, d), lambda i, j: (j, 0)),
                  pl.BlockSpec((d, te), lambda i, j: (0, j)),
                  per_tile, per_tile, small, small],
        out_specs=pl.BlockSpec((d, tn), lambda i, j: (0, i)),
        scratch_shapes=[pltpu.VMEM((te, tn), F32), pltpu.VMEM((te, tn), BF16)],
        compiler_params=_params("parallel", "arbitrary"),
        name="peermain",
    )(h2t, u_bf16, vt_bf16, c1, a1, r2, a2)


def _ln2_kernel(x1_ref, fft_ref, g2_ref, lg_ref, lb_ref, y_ref, *, alpha):
    ff = fft_ref[...].T
    y_ref[...] = _layernorm_rows(alpha * x1_ref[...] + (1.0 + g2_ref[...]) * ff, lg_ref[...], lb_ref[...])


def _ln2(x1, fft, col0, ms, ln_g, ln_b, alpha):
    n, d = x1.shape
    tm = ms.tm
    c0 = col0 // tm
    return pl.pallas_call(
        functools.partial(_ln2_kernel, alpha=alpha),
        out_shape=jax.ShapeDtypeStruct((n, d), F32),
        grid=(n // tm,),
        in_specs=[pl.BlockSpec((tm, d), lambda i: (i, 0)),
                  pl.BlockSpec((d, tm), lambda i: (0, c0 + i)),
                  ms.spec(5, 1),
                  pl.BlockSpec((1, d), lambda i: (0, 0)), pl.BlockSpec((1, d), lambda i: (0, 0))],
        out_specs=pl.BlockSpec((tm, d), lambda i: (i, 0)),
        compiler_params=_params("parallel"),
        name="ln2",
    )(x1, fft, ms.arr, ln_g.reshape(1, d), ln_b.reshape(1, d))


def _pick(n, pref):
    for t in pref:
        if n % t == 0:
            return t
    raise ValueError(f"no tile for {n}")


def kernel(x_prompt, x_sample, cache_k, cache_v, state_hgrn, page_table, c_prompt, c_sample, w_ada, b_ada, w_in, lb_raw, norm_a, norm_b, sb_bias, w_out, ln1_g, ln1_b, ln2_g, ln2_b, peer_wq, peer_k1, peer_k2, peer_u, peer_v):
    depth = w_ada.shape[0]
    bp, tp, d = x_prompt.shape
    bs, ts, _ = x_sample.shape
    n_p, n_s = bp * tp, bs * ts
    page, hb = cache_k.shape[2], cache_k.shape[3]
    wb = hb * HEAD_DIM
    alpha = (2.0 * depth) ** 0.25
    lb_all = jnp.cumsum(jax.nn.softmax(lb_raw.astype(F32), axis=0), axis=0)

    tm_p = _pick(tp, (1024, 512, 256, 128))
    tm_s = _pick(n_s, (1024, 512, 256, 128))
    assert n_p % tm_s == 0
    n_all = n_p + n_s
    tn_k = _pick(n_all, (256, 128))
    tn_m = _pick(n_all, (512, 256, 128))
    tq = _pick(tp, (512, 256, 128))
    tk = min(tq, 256)

    yp = x_prompt.reshape(n_p, d)
    ys = x_sample.reshape(n_s, d)
    outs = {k: [] for k in ("kp", "vp", "sp", "ks", "vs", "ss")}
    for l in range(depth):
        mod = _mod(jnp.concatenate([c_prompt, c_sample], axis=0), w_ada[l], b_ada[l])
        ms_p = _ModSpec(mod[:bp].reshape(bp, 1, 6 * d), False, tm_p, tp // tm_p, d)
        ms_s = _ModSpec(jnp.repeat(mod[bp:], ts, axis=0), True, tm_s, 1, d)
        w_in_b = w_in[l].astype(BF16)
        w_out_b = w_out[l].astype(BF16)
        lb = lb_all[l]

        proj_p = _inproj(yp, ms_p, w_in_b, wb)
        proj_s = _inproj(ys, ms_s, w_in_b, wb)

        oa_p, sp = _hgrn(proj_p, lb, None, bp, tp)
        oa_s, ss = _hgrn(proj_s, lb, state_hgrn[l], bs, ts)

        ob_p = _sb_prompt(proj_p, sb_bias[l].astype(F32), bp, tp, tq, tk)
        q_hi = proj_s[4].reshape(bs, ts, hb, HEAD_DIM).transpose(0, 2, 1, 3).reshape(bs, hb * ts, HEAD_DIM)
        ob_s = _sb_paged(q_hi, proj_s[5].reshape(bs, ts * hb, HEAD_DIM), proj_s[6].reshape(bs, ts * hb, HEAD_DIM),
                         cache_k, cache_v, l, page_table, sb_bias[l])
        ob_s = ob_s.reshape(bs, hb, ts, HEAD_DIM).transpose(0, 2, 1, 3).reshape(n_s, wb)

        x1_p, h2t_p = _outproj(oa_p, proj_p, ob_p, yp, ms_p, norm_a[l], norm_b[l], w_out_b, ln1_g[l], ln1_b[l], alpha)
        x1_s, h2t_s = _outproj(oa_s, proj_s, ob_s, ys, ms_s, norm_a[l], norm_b[l], w_out_b, ln1_g[l], ln1_b[l], alpha)

        h2t = jnp.concatenate([h2t_p, h2t_s], axis=1)
        c1, a1, r2, a2 = _peer_topk(h2t, peer_wq[l].T.astype(BF16), peer_k1[l], peer_k2[l], tn_k)
        fft = _peer_main(h2t, peer_u[l].astype(BF16), peer_v[l].T.astype(BF16), c1, a1, r2, a2,
                         tn_m, 8 * peer_k1.shape[1])

        yp = _ln2(x1_p, fft, 0, ms_p, ln2_g[l], ln2_b[l], alpha)
        ys = _ln2(x1_s, fft, n_p, ms_s, ln2_g[l], ln2_b[l], alpha)

        outs["kp"].append(proj_p[5].reshape(bp, tp, hb, HEAD_DIM))
        outs["vp"].append(proj_p[6].reshape(bp, tp, hb, HEAD_DIM))
        outs["sp"].append(sp)
        outs["ks"].append(proj_s[5].reshape(bs, ts, hb, HEAD_DIM))
        outs["vs"].append(proj_s[6].reshape(bs, ts, hb, HEAD_DIM))
        outs["ss"].append(ss)
    return (yp.reshape(bp, tp, d), ys.reshape(bs, ts, d),
            jnp.stack(outs["kp"]), jnp.stack(outs["vp"]), jnp.stack(outs["sp"]),
            jnp.stack(outs["ks"]), jnp.stack(outs["vs"]), jnp.stack(outs["ss"]))
```

```python
import functools
import math

import numpy as np
import jax
import jax.numpy as jnp
from jax import lax
from jax.experimental import pallas as pl
from jax.experimental.pallas import tpu as pltpu

F32 = jnp.float32
BF16 = jnp.bfloat16

HEAD_DIM = 64
LANES = 128
LN_EPS = 1e-5
RMS_EPS = 1e-6
PEER_TOPK = 16
VMEM_LIMIT = 56 * 1024 * 1024

_NT = (((1,), (1,)), ((), ()))
_TN = (((0,), (0,)), ((), ()))


def _dot(a, b):
    return jnp.dot(a, b, preferred_element_type=F32)


def _dot_nt(a, b):
    return lax.dot_general(a, b, _NT, preferred_element_type=F32)


def _split3(x):
    hi = x.astype(BF16)
    r1 = x - hi.astype(F32)
    mid = r1.astype(BF16)
    lo = (r1 - mid.astype(F32)).astype(BF16)
    return hi, mid, lo


def _params(*sem):
    return pltpu.CompilerParams(dimension_semantics=sem, vmem_limit_bytes=VMEM_LIMIT)


def _mod_kernel(c_ref, w_ref, b_ref, o_ref):
    o_ref[...] = _dot(c_ref[...].astype(BF16), w_ref[...].astype(BF16)) + b_ref[...]


def _mod(c, w, b):
    n, d = c.shape
    width = w.shape[1]
    tn = 1024
    return pl.pallas_call(
        _mod_kernel,
        out_shape=jax.ShapeDtypeStruct((n, width), F32),
        grid=(width // tn,),
        in_specs=[pl.BlockSpec((n, d), lambda j: (0, 0)),
                  pl.BlockSpec((d, tn), lambda j: (0, j)),
                  pl.BlockSpec((1, tn), lambda j: (0, j))],
        out_specs=pl.BlockSpec((n, tn), lambda j: (0, j)),
        compiler_params=_params("parallel"),
        name="mod",
    )(c, w, b.reshape(1, width))


class _ModSpec:
    def __init__(self, arr, per_token, tm, tiles_per_seq, d):
        self.arr, self.per_token, self.tm, self.tps, self.d = arr, per_token, tm, tiles_per_seq, d

    def spec(self, part, ngrid):
        d, tps = self.d, self.tps
        if self.per_token:
            if ngrid == 1:
                return pl.BlockSpec((self.tm, d), lambda i: (i, part))
            return pl.BlockSpec((self.tm, d), lambda i, j: (i, part))
        if ngrid == 1:
            return pl.BlockSpec((None, 1, d), lambda i: (i // tps, 0, part))
        return pl.BlockSpec((None, 1, d), lambda i, j: (i // tps, 0, part))


def _inproj_kernel(x_ref, sh_ref, sc_ref, w_ref, o_ref, h_ref):
    @pl.when(pl.program_id(1) == 0)
    def _():
        h_ref[...] = (x_ref[...] * (1.0 + sc_ref[...]) + sh_ref[...]).astype(BF16)

    o_ref[...] = _dot(h_ref[...], w_ref[...])


def _inproj(x, ms, w_bf16, tn):
    n, d = x.shape
    tm = ms.tm
    nparts = w_bf16.shape[1] // tn
    return pl.pallas_call(
        _inproj_kernel,
        out_shape=jax.ShapeDtypeStruct((nparts, n, tn), F32),
        grid=(n // tm, nparts),
        in_specs=[pl.BlockSpec((tm, d), lambda i, j: (i, 0)),
                  ms.spec(0, 2), ms.spec(1, 2),
                  pl.BlockSpec((d, tn), lambda i, j: (0, j))],
        out_specs=pl.BlockSpec((None, tm, tn), lambda i, j: (j, i, 0)),
        scratch_shapes=[pltpu.VMEM((tm, d), BF16)],
        compiler_params=_params("parallel", "arbitrary"),
        name="inproj",
    )(x, ms.arr, ms.arr, w_bf16)


def _hgrn_consts(c, lseq):
    nlev = int(math.log2(lseq))
    r = np.arange(c)
    seg, pos = r // lseq, r % lseq
    j = np.arange(c)
    same = seg[:, None] == seg[None, :]
    mats = [same & (j[None, :] <= r[:, None]),
            same & (j[None, :] > r[:, None])]
    rq = np.zeros((nlev, c, 1), np.float32)
    mb = np.zeros((nlev + 1, c, c), np.float32)
    for lev in range(nlev):
        half, blk = 1 << lev, 2 << lev
        is_q = (pos % blk) >= half
        ref = (r // blk) * blk + half - 1
        m = np.where(is_q[:, None],
                     (j[None, :] > ref[:, None]) & (j[None, :] <= r[:, None]),
                     (j[None, :] > r[:, None]) & (j[None, :] <= ref[:, None]))
        mats.append(m)
        rq[lev, :, 0] = is_q
        mb[lev] = (r[:, None] // blk) == (r[None, :] // blk)
    mb[nlev] = np.eye(c)
    a = np.concatenate(mats, axis=0).astype(np.float32)
    rq = np.broadcast_to(rq, (nlev, c, LANES)).copy()
    mb2 = np.concatenate([mb, mb], axis=1)
    a2 = np.concatenate([a, a], axis=1)
    return jnp.asarray(a2, BF16), jnp.asarray(rq), jnp.asarray(mb2), nlev


def _hgrn_kernel(*refs, c, lseq, nch, nlev, has_s0):
    if has_s0:
        q_ref, f_ref, i_ref, lb_ref, a_ref, rq_ref, mb_ref, s0_ref, o_ref, s_ref, st_ref = refs
    else:
        q_ref, f_ref, i_ref, lb_ref, a_ref, rq_ref, mb_ref, o_ref, s_ref, st_ref = refs
    nsq = c // lseq
    lbv = lb_ref[...]
    lane_lo = lax.broadcasted_iota(jnp.int32, (c, LANES), 1) < HEAD_DIM
    r128 = lax.broadcasted_iota(jnp.int32, (LANES, nsq * LANES), 0)
    c128 = lax.broadcasted_iota(jnp.int32, (LANES, nsq * LANES), 1) % LANES
    same_head = (r128 < HEAD_DIM) == (c128 < HEAD_DIM)
    row_seq = lax.broadcasted_iota(jnp.int32, (c, LANES), 0) // lseq
    zero64 = jnp.zeros((HEAD_DIM, HEAD_DIM), F32)

    if has_s0:
        for s in range(nsq):
            top = jnp.concatenate([s0_ref[s, 0], zero64], axis=1)
            bot = jnp.concatenate([zero64, s0_ref[s, 1]], axis=1)
            st_ref[:, s * LANES:(s + 1) * LANES] = jnp.concatenate([top, bot], axis=0).T
    else:
        st_ref[...] = jnp.zeros_like(st_ref)

    def expand(x):
        if nsq == 1:
            return x
        return jnp.concatenate([jnp.where(row_seq == s, x, 0.0) for s in range(nsq)], axis=1)

    def heads_on_rows(x):
        return jnp.concatenate([jnp.where(lane_lo, x, 0.0), jnp.where(lane_lo, 0.0, x)], axis=0)

    def chunk(ci, carry):
        r0 = pl.multiple_of(ci * c, c)
        qa = q_ref[pl.ds(r0, c), :]
        z = f_ref[pl.ds(r0, c), :]
        v = i_ref[pl.ds(r0, c), :]
        lf = jnp.log(lbv + (1.0 - lbv) * jax.nn.sigmoid(z))
        kin = (1.0 - lbv) * jax.nn.sigmoid(-z)
        qs = qa * jax.nn.sigmoid(qa)
        hi = lf.astype(BF16)
        lo = (lf - hi.astype(F32)).astype(BF16)
        g = _dot(a_ref[...], jnp.concatenate([hi, lo], axis=0))
        eb = jnp.exp(g[0:c])
        qg = qs * eb
        kd = kin * jnp.exp(g[c:2 * c])
        st = st_ref[...]
        o = _dot_nt(expand(qg).astype(BF16), st.astype(BF16))

        kin_b = kin.astype(BF16)
        sc = mb_ref[nlev] * _dot_nt(heads_on_rows(qs).astype(BF16), kin_b)
        for lev in range(nlev):
            y = jnp.exp(g[(2 + lev) * c:(3 + lev) * c])
            isq = rq_ref[lev]
            ql = qs * (y * isq)
            kl = kin * (y * (1.0 - isq))
            sc = sc + mb_ref[lev] * _dot_nt(heads_on_rows(ql).astype(BF16), kl.astype(BF16))
        r = _dot(sc.astype(BF16), v.astype(BF16))
        o_ref[pl.ds(r0, c), :] = o + jnp.where(lane_lo, r[0:c], r[c:2 * c])

        if nsq == 1:
            dl = eb[c - 1:c, :]
        else:
            dl = jnp.concatenate([eb[(s + 1) * lseq - 1:(s + 1) * lseq, :] for s in range(nsq)], axis=1)
        u = _dot(v.T.astype(BF16), expand(kd).astype(BF16))
        st_ref[...] = st * dl + jnp.where(same_head, u, 0.0)
        return carry

    lax.fori_loop(0, nch, chunk, 0)

    for s in range(nsq):
        sbd = st_ref[:, s * LANES:(s + 1) * LANES].T
        s_ref[s, 0] = sbd[0:HEAD_DIM, 0:HEAD_DIM]
        s_ref[s, 1] = sbd[HEAD_DIM:, HEAD_DIM:]


def _hgrn(proj, lb, s0, nseq, t):
    _, n, width = proj.shape
    npair = width // LANES
    c = LANES
    lseq = min(t, c)
    nsq = c // lseq
    if t >= c:
        nch, rows, ngroups = t // c, t, nseq
    else:
        nch, rows, ngroups = 1, c, nseq // nsq
    a, rq, mb, nlev = _hgrn_consts(c, lseq)
    has_s0 = s0 is not None
    ins = [proj, proj, proj, lb.reshape(1, width), a, rq, mb]
    in_specs = [pl.BlockSpec((None, rows, LANES), lambda g, p: (0, g, p)),
                pl.BlockSpec((None, rows, LANES), lambda g, p: (1, g, p)),
                pl.BlockSpec((None, rows, LANES), lambda g, p: (2, g, p)),
                pl.BlockSpec((1, LANES), lambda g, p: (0, p)),
                pl.BlockSpec(a.shape, lambda g, p: (0, 0)),
                pl.BlockSpec(rq.shape, lambda g, p: (0, 0, 0)),
                pl.BlockSpec(mb.shape, lambda g, p: (0, 0, 0))]
    if has_s0:
        ins.append(s0)
        in_specs.append(pl.BlockSpec((nsq, 2, HEAD_DIM, HEAD_DIM), lambda g, p: (g, p, 0, 0)))
    kern = functools.partial(_hgrn_kernel, c=c, lseq=lseq, nch=nch, nlev=nlev, has_s0=has_s0)
    return pl.pallas_call(
        kern,
        out_shape=(jax.ShapeDtypeStruct((n, width), F32),
                   jax.ShapeDtypeStruct((nseq, 2 * npair, HEAD_DIM, HEAD_DIM), F32)),
        grid=(ngroups, npair),
        in_specs=in_specs,
        out_specs=(pl.BlockSpec((rows, LANES), lambda g, p: (g, p)),
                   pl.BlockSpec((nsq, 2, HEAD_DIM, HEAD_DIM), lambda g, p: (g, p, 0, 0))),
        scratch_shapes=[pltpu.VMEM((LANES, nsq * LANES), F32)],
        compiler_params=_params("parallel", "parallel"),
        name="hgrn",
    )(*ins)


LOG2E = 1.4426950408889634


def _sb_block(z2, tail, uneg, mask):
    sp = jnp.maximum(z2, 0.0) + jnp.log2(1.0 + jnp.exp2(-jnp.abs(z2)))
    if mask is not None:
        sp = jnp.where(mask, sp, 0.0)
    rr = _dot(sp.astype(BF16), uneg)
    a = jnp.exp2(z2 - sp + (tail + rr))
    if mask is not None:
        a = jnp.where(mask, a, 0.0)
    return a, tail - jnp.sum(sp, axis=1, keepdims=True)


def _uneg_const(tk):
    j = np.arange(tk)
    return jnp.asarray(-(j[:, None] > j[None, :]).astype(np.float32), BF16)


def _sb_kernel(q_ref, k_ref, v_ref, bias_ref, u2_ref, o_ref, *, tq, tk):
    qi = pl.program_id(2)
    u2 = u2_ref[...]
    scale = HEAD_DIM ** -0.5 * LOG2E
    lane_lo = lax.broadcasted_iota(jnp.int32, (tq, LANES), 1) < HEAD_DIM
    q = q_ref[...] * scale
    rows = qi * tq + lax.broadcasted_iota(jnp.int32, (tq, tk), 0)
    cols = lax.broadcasted_iota(jnp.int32, (tq, tk), 1)
    ndiag = tq // tk
    qh = (jnp.where(lane_lo, q, 0.0).astype(BF16), jnp.where(lane_lo, 0.0, q).astype(BF16))
    bias = (bias_ref[2 * pl.program_id(1)] * LOG2E, bias_ref[2 * pl.program_id(1) + 1] * LOG2E)

    def block(kb, carry, masked):
        k0 = pl.multiple_of(kb * tk, tk)
        kk = k_ref[pl.ds(k0, tk), :].astype(BF16)
        vv = v_ref[pl.ds(k0, tk), :].astype(BF16)
        mask = (cols + kb * tk < rows) if masked else None
        out = []
        for h in range(2):
            acc, tail = carry[h]
            a, tail = _sb_block(_dot_nt(qh[h], kk) + bias[h], tail, u2, mask)
            out.append((acc + _dot(a.astype(BF16), vv), tail))
        return tuple(out)

    zero = (jnp.zeros((tq, LANES), F32), jnp.zeros((tq, 1), F32))
    carry = (zero, zero)
    for d in range(ndiag):
        carry = block(qi * ndiag + (ndiag - 1 - d), carry, True)
    carry = lax.fori_loop(0, qi * ndiag, lambda i, c: block(qi * ndiag - 1 - i, c, False), carry)
    o_ref[...] = jnp.where(lane_lo, carry[0][0], carry[1][0])


def _sb_prompt(proj, bias, nseq, t, tq, tk):
    _, n, width = proj.shape
    npair = width // LANES
    nq = t // tq
    u2 = _uneg_const(tk)
    kern = functools.partial(_sb_kernel, tq=tq, tk=tk)
    return pl.pallas_call(
        kern,
        out_shape=jax.ShapeDtypeStruct((n, width), F32),
        grid=(nseq, npair, nq),
        in_specs=[pl.BlockSpec((None, tq, LANES), lambda b, p, i: (4, b * nq + i, p)),
                  pl.BlockSpec((None, t, LANES), lambda b, p, i: (5, b, p)),
                  pl.BlockSpec((None, t, LANES), lambda b, p, i: (6, b, p)),
                  pl.BlockSpec(memory_space=pltpu.SMEM),
                  pl.BlockSpec(u2.shape, lambda b, p, i: (0, 0))],
        out_specs=pl.BlockSpec((tq, LANES), lambda b, p, i: (b * nq + i, p)),
        compiler_params=_params("parallel", "parallel", "arbitrary"),
        name="sb",
    )(proj, proj, proj, bias, u2)


def _sbpaged_kernel(pt_ref, q_ref, kn_ref, vn_ref, bias_ref, u2_ref, *rest, npages, page, t, nh):
    k_refs = rest[:npages]
    v_refs = rest[npages:2 * npages]
    o_ref = rest[2 * npages]
    rows = nh * t
    scale = HEAD_DIM ** -0.5 * LOG2E
    u2 = u2_ref[...]
    qa = (q_ref[...] * scale).astype(BF16)
    bias = bias_ref[...]
    head_z = lax.broadcasted_iota(jnp.int32, (rows, page), 0) // t
    head_o = lax.broadcasted_iota(jnp.int32, (rows, HEAD_DIM), 0) // t

    order = list(range(npages - 1, -1, -1))
    kblk = [kn_ref] + [k_refs[p] for p in order]
    vblk = [vn_ref] + [v_refs[p] for p in order]
    nblk = npages + 1

    def logits(x_ref):
        z = bias
        for h in range(nh):
            z = z + jnp.where(head_z == h, _dot(qa, x_ref[h].astype(BF16)), 0.0)
        return z

    z2 = jnp.concatenate([logits(x) for x in kblk], axis=0)
    qpos = lax.broadcasted_iota(jnp.int32, (rows, page), 0) % t
    kpos = lax.broadcasted_iota(jnp.int32, (rows, page), 1)
    visible = jnp.concatenate([kpos < qpos] + [jnp.ones((rows, page), jnp.bool_)] * npages, axis=0)
    sp = jnp.where(visible, jnp.maximum(z2, 0.0) + jnp.log2(1.0 + jnp.exp2(-jnp.abs(z2))), 0.0)
    rr = _dot(sp.astype(BF16), u2)
    blk_sum = jnp.sum(sp, axis=1, keepdims=True)
    tails = [jnp.zeros((rows, 1), F32)]
    for b in range(nblk - 1):
        tails.append(tails[-1] - blk_sum[b * rows:(b + 1) * rows])
    a = jnp.where(visible, jnp.exp2(z2 - sp + (jnp.concatenate(tails, axis=0) + rr)), 0.0).astype(BF16)
    acc = jnp.zeros((rows, HEAD_DIM), F32)
    for b in range(nblk):
        a_b = a[b * rows:(b + 1) * rows]
        for h in range(nh):
            acc = acc + jnp.where(head_o == h, _dot_nt(a_b, vblk[b][h].astype(BF16)), 0.0)
    o_ref[...] = acc


def _sb_paged(q, kn_t, vn_t, cache_kt, cache_vt, layer, page_table, sb_bias):
    nseq, rows, _ = q.shape
    nh, page = cache_kt.shape[2], cache_kt.shape[4]
    t = rows // nh
    npages = page_table.shape[1]
    u2 = _uneg_const(page)
    bias = jnp.broadcast_to(jnp.repeat(sb_bias.astype(F32) * LOG2E, t)[:, None], (rows, page))
    kern = functools.partial(_sbpaged_kernel, npages=npages, page=page, t=t, nh=nh)

    def page_spec(p):
        return pl.BlockSpec((None, None, nh, HEAD_DIM, page), lambda b, pt: (layer, pt[b, p], 0, 0, 0))

    new_spec = pl.BlockSpec((None, nh, HEAD_DIM, page), lambda b, pt: (b, 0, 0, 0))
    grid_spec = pltpu.PrefetchScalarGridSpec(
        num_scalar_prefetch=1,
        grid=(nseq,),
        in_specs=[pl.BlockSpec((None, rows, HEAD_DIM), lambda b, pt: (b, 0, 0)),
                  new_spec, new_spec,
                  pl.BlockSpec((rows, page), lambda b, pt: (0, 0)),
                  pl.BlockSpec(u2.shape, lambda b, pt: (0, 0))]
                 + [page_spec(p) for p in range(npages)] * 2,
        out_specs=pl.BlockSpec((None, rows, HEAD_DIM), lambda b, pt: (b, 0, 0)),
    )
    return pl.pallas_call(
        kern,
        out_shape=jax.ShapeDtypeStruct((nseq, rows, HEAD_DIM), F32),
        grid_spec=grid_spec,
        compiler_params=_params("arbitrary"),
        name="sbpaged",
    )(page_table, q, kn_t, vn_t, bias, u2, *([cache_kt] * npages), *([cache_vt] * npages))


def _layernorm_rows(x, g, b):
    mu = jnp.mean(x, axis=-1, keepdims=True)
    xc = x - mu
    var = jnp.mean(xc * xc, axis=-1, keepdims=True)
    return xc * lax.rsqrt(var + LN_EPS) * g + b


def _head_rms(o, hsum):
    sq = o * o
    hi = sq.astype(BF16)
    lo = (sq - hi.astype(F32)).astype(BF16)
    ms = (_dot(hi, hsum) + _dot(lo, hsum)) * (1.0 / HEAD_DIM)
    return o * lax.rsqrt(ms + RMS_EPS)


def _outproj_kernel(oa_ref, ga_ref, ob_ref, x_ref, g1_ref, sh2_ref, sc2_ref, na_ref, nb_ref, hs_ref,
                    w_ref, lg_ref, lb_ref, x1_ref, h2t_ref, *, alpha):
    hs = hs_ref[...]
    ga = ga_ref[...]
    ya = _head_rms(oa_ref[...], hs) * na_ref[...] * (ga * jax.nn.sigmoid(ga))
    yb = _head_rms(ob_ref[...], hs) * nb_ref[...]
    wa = w_ref.shape[0] // 2
    mix = _dot(ya.astype(BF16), w_ref[0:wa, :]) + _dot(yb.astype(BF16), w_ref[wa:, :])
    x1 = _layernorm_rows(alpha * x_ref[...] + (1.0 + g1_ref[...]) * mix, lg_ref[...], lb_ref[...])
    x1_ref[...] = x1
    h2 = x1 * (1.0 + sc2_ref[...]) + sh2_ref[...]
    h2t_ref[...] = h2.T.astype(BF16)


def _outproj(oa, proj, ob, x, ms, norm_a, norm_b, w_out_bf16, ln_g, ln_b, alpha):
    n, d = x.shape
    tm = ms.tm
    wa = oa.shape[1]
    hs = (np.arange(wa)[:, None] // HEAD_DIM == np.arange(wa)[None, :] // HEAD_DIM).astype(np.float32)
    row = lambda i: (i, 0)
    fixed = lambda i: (0, 0)
    return pl.pallas_call(
        functools.partial(_outproj_kernel, alpha=alpha),
        out_shape=(jax.ShapeDtypeStruct((n, d), F32), jax.ShapeDtypeStruct((d, n), BF16)),
        grid=(n // tm,),
        in_specs=[pl.BlockSpec((tm, wa), row),
                  pl.BlockSpec((None, tm, wa), lambda i: (3, i, 0)),
                  pl.BlockSpec((tm, wa), row),
                  pl.BlockSpec((tm, d), row),
                  ms.spec(2, 1), ms.spec(3, 1), ms.spec(4, 1),
                  pl.BlockSpec((1, wa), fixed), pl.BlockSpec((1, wa), fixed),
                  pl.BlockSpec((wa, wa), fixed),
                  pl.BlockSpec((2 * wa, d), fixed),
                  pl.BlockSpec((1, d), fixed), pl.BlockSpec((1, d), fixed)],
        out_specs=(pl.BlockSpec((tm, d), row), pl.BlockSpec((d, tm), lambda i: (0, i))),
        compiler_params=_params("parallel"),
        name="outproj",
    )(oa, proj, ob, x, ms.arr, ms.arr, ms.arr, norm_a.reshape(1, wa), norm_b.reshape(1, wa),
      jnp.asarray(hs, BF16), w_out_bf16, ln_g.reshape(1, d), ln_b.reshape(1, d))


MARK0 = -(2.0 ** 127)
MSTEP = 2.0 ** 105
CAND_ROWS = 64


def _stair_cells():
    k = PEER_TOPK
    return [(a, b) for a in range(k) for b in range(k) if (a + 1) * (b + 1) <= k]


def _cand_consts():
    cells = _stair_cells()
    g1 = np.zeros((CAND_ROWS, PEER_TOPK), np.float32)
    g2 = np.zeros((CAND_ROWS, PEER_TOPK), np.float32)
    for r, (a, b) in enumerate(cells):
        g1[r, a] = 1.0
        g2[r, b] = 1.0
    return jnp.asarray(g1, BF16), jnp.asarray(g2, BF16), jnp.asarray(g1.T.copy(), BF16), len(cells)


def _extract_topk(s, k):
    rows, tn = s.shape
    ridx = lax.broadcasted_iota(jnp.int32, (rows, tn), 0)
    kidx = lax.broadcasted_iota(jnp.int32, (k, tn), 0)

    def body(i, carry):
        cur, vals, rank = carry
        m = jnp.max(cur, axis=0, keepdims=True)
        first = jnp.min(jnp.where(cur == m, ridx, rows), axis=0, keepdims=True)
        hit = ridx == first
        cur = jnp.where(hit, -jnp.inf, cur)
        rank = jnp.where(hit, jnp.full((1, tn), i, jnp.int32).astype(F32), rank)
        vals = jnp.where(kidx == i, m, vals)
        return cur, vals, rank

    init = (s, jnp.zeros((k, tn), F32), jnp.full((rows, tn), float(k), F32))
    _, vals, rank = lax.fori_loop(0, k, body, init)
    return vals, rank


def _knock_topk(cur_ref, val_ref, narr, k):
    tn = cur_ref.shape[2]

    def body(i, carry):
        mark = MARK0 - jnp.full((1, tn), i, jnp.int32).astype(F32) * MSTEP
        for a in range(narr):
            cur = cur_ref[a]
            m = jnp.max(cur, axis=0, keepdims=True)
            cur_ref[a] = jnp.where(cur == m, mark, cur)
            val_ref[a, pl.ds(i, 1), :] = m
        return carry

    lax.fori_loop(0, k, body, 0)


def _tie_free(cur_ref, narr, k, valid=None):
    ok = None
    for a in range(narr):
        gone = cur_ref[a] <= MARK0
        if valid is not None:
            gone = gone & valid
        good = jnp.sum(gone.astype(F32), axis=0, keepdims=True) == float(k)
        ok = good if ok is None else (ok & good)
    return jnp.min(ok.astype(jnp.int32)) == 1


def _exact_topk_into(src_ref, cur_ref, val_ref, narr, k):
    for a in range(narr):
        s = src_ref[a]
        vals, rank = _extract_topk(s, k)
        val_ref[a] = vals
        cur_ref[a] = jnp.where(rank < float(k), MARK0 - rank * MSTEP, s)


def _bcast_rows_bf16(row, nrows):
    tile = jnp.broadcast_to(row, (16, row.shape[1])).astype(BF16)
    return jnp.concatenate([tile] * (nrows // 16), axis=0)


def _peertopk_kernel(ht_ref, wqt_ref, k1_ref, k2_ref, g1_ref, g2_ref, ga_ref, c1_ref, a1_ref, r2_ref, a2_ref,
                     s_scr, cur_scr, val_scr, cand_scr, ccur_scr, cval_scr, *, nheads, nkeys, ncells):
    tn = ht_ref.shape[1]
    dq = k1_ref.shape[1]
    k = PEER_TOPK
    qt = _dot(wqt_ref[...], ht_ref[...]).astype(BF16)
    k1 = k1_ref[...].astype(BF16)
    k2 = k2_ref[...].astype(BF16)
    for h in range(nheads):
        s_scr[2 * h] = _dot(k1, qt[(2 * h) * dq:(2 * h + 1) * dq, :])
        s_scr[2 * h + 1] = _dot(k2, qt[(2 * h + 1) * dq:(2 * h + 2) * dq, :])
    cur_scr[...] = s_scr[...]
    _knock_topk(cur_scr, val_scr, 2 * nheads, k)

    @pl.when(jnp.logical_not(_tie_free(cur_scr, 2 * nheads, k)))
    def _():
        _exact_topk_into(s_scr, cur_scr, val_scr, 2 * nheads, k)

    g1, g2 = g1_ref[...], g2_ref[...]
    valid = lax.broadcasted_iota(jnp.int32, (CAND_ROWS, tn), 0) < ncells

    def pick(g, v):
        hi, mid, lo = _split3(v)
        return _dot(g, hi) + _dot(g, mid) + _dot(g, lo)

    for h in range(nheads):
        cand = jnp.where(valid, pick(g1, val_scr[2 * h]) + pick(g2, val_scr[2 * h + 1]), -jnp.inf)
        cand_scr[h] = cand
        ccur_scr[h] = cand
    _knock_topk(ccur_scr, cval_scr, nheads, k)

    @pl.when(jnp.logical_not(_tie_free(ccur_scr, nheads, k, valid)))
    def _():
        _exact_topk_into(cand_scr, ccur_scr, cval_scr, nheads, k)

    ga = ga_ref[...]
    inv_step = 1.0 / MSTEP
    for h in range(nheads):
        taken = ((ccur_scr[h] <= MARK0) & valid).astype(BF16)
        cnt = _dot(ga, taken)
        top = cval_scr[h]
        inv_z = 1.0 / jnp.sum(jnp.exp(top - top[0:1]), axis=0, keepdims=True)
        s1, s2 = s_scr[2 * h], s_scr[2 * h + 1]
        cur1, cur2 = cur_scr[2 * h], cur_scr[2 * h + 1]
        sel1, sel2 = cur1 <= MARK0, cur2 <= MARK0
        rank1 = jnp.where(sel1, (MARK0 - cur1) * inv_step, float(k)).astype(BF16)
        rank2 = jnp.where(sel2, (MARK0 - cur2) * inv_step, float(k))
        c1 = jnp.zeros((nkeys, tn), BF16)
        for a in range(k):
            c1 = jnp.where(rank1 == a, _bcast_rows_bf16(cnt[a:a + 1], nkeys), c1)
        v1 = val_scr[2 * h]
        v2 = val_scr[2 * h + 1]
        c1_ref[h] = c1.astype(F32)
        a1_ref[h] = jnp.where(sel1, jnp.exp(s1 - v1[0:1]) * inv_z, 0.0)
        r2_ref[h] = rank2.astype(BF16)
        a2_ref[h] = jnp.where(sel2, jnp.exp(s2 - v2[0:1]), 0.0).astype(BF16)


def _peer_topk(h2t, wqt_bf16, k1, k2, tn):
    d, n = h2t.shape
    nkeys, dq = k1.shape
    nheads = wqt_bf16.shape[0] // (2 * dq)
    g1, g2, ga, ncells = _cand_consts()
    shp = (nheads, nkeys, n)
    blk = pl.BlockSpec((nheads, nkeys, tn), lambda i: (0, 0, i))
    fixed = lambda i: (0, 0)
    k = PEER_TOPK
    return pl.pallas_call(
        functools.partial(_peertopk_kernel, nheads=nheads, nkeys=nkeys, ncells=ncells),
        out_shape=(jax.ShapeDtypeStruct(shp, F32), jax.ShapeDtypeStruct(shp, F32),
                   jax.ShapeDtypeStruct(shp, BF16), jax.ShapeDtypeStruct(shp, BF16)),
        grid=(n // tn,),
        in_specs=[pl.BlockSpec((d, tn), lambda i: (0, i)),
                  pl.BlockSpec(wqt_bf16.shape, fixed),
                  pl.BlockSpec((nkeys, dq), fixed), pl.BlockSpec((nkeys, dq), fixed),
                  pl.BlockSpec(g1.shape, fixed), pl.BlockSpec(g2.shape, fixed), pl.BlockSpec(ga.shape, fixed)],
        out_specs=(blk, blk, blk, blk),
        scratch_shapes=[pltpu.VMEM((2 * nheads, nkeys, tn), F32), pltpu.VMEM((2 * nheads, nkeys, tn), F32),
                        pltpu.VMEM((2 * nheads, k, tn), F32),
                        pltpu.VMEM((nheads, CAND_ROWS, tn), F32), pltpu.VMEM((nheads, CAND_ROWS, tn), F32),
                        pltpu.VMEM((nheads, k, tn), F32)],
        compiler_params=_params("parallel"),
        name="peertopk",
    )(h2t, wqt_bf16, k1, k2, g1, g2, ga)


GATE_LANES = 256


def _gelu_tanh(x):
    c = 0.7978845608028654
    hx = 0.5 * x
    return hx + hx * jnp.tanh(x * (c + (c * 0.044715) * (x * x)))


def _peermain_kernel(ht_ref, u_ref, vt_ref, c1_ref, a1_ref, r2_ref, a2_ref, o_ref, act_ref, g_ref, *, nkeys, nheads):
    j = pl.program_id(1)
    te, tn = act_ref.shape
    gl = min(GATE_LANES, tn)
    assert tn % gl == 0
    zero = jnp.zeros((), BF16)

    @pl.when(j == 0)
    def _():
        o_ref[...] = jnp.zeros_like(o_ref)

    act_ref[...] = _dot(u_ref[...], ht_ref[...])
    for sub in range(te // nkeys):
        arow = slice(sub * nkeys, (sub + 1) * nkeys)
        for lt in range(tn // gl):
            cols = slice(lt * gl, (lt + 1) * gl)
            w = jnp.zeros((nkeys, gl), BF16)
            for h in range(nheads):
                c1 = _bcast_rows_bf16(c1_ref[h, sub:sub + 1, cols], nkeys)
                a1 = _bcast_rows_bf16(a1_ref[h, sub:sub + 1, cols], nkeys)
                w = w + a1 * jnp.where(r2_ref[h, :, cols] < c1, a2_ref[h, :, cols], zero)
            g_ref[arow, cols] = w * _gelu_tanh(act_ref[arow, cols]).astype(BF16)
    o_ref[...] += _dot(vt_ref[...], g_ref[...])


def _peer_main(h2t, u_bf16, vt_bf16, c1, a1, r2, a2, tn, te):
    d, n = h2t.shape
    ne = u_bf16.shape[0]
    nheads, nkeys, _ = c1.shape
    small = pl.BlockSpec((nheads, nkeys, tn), lambda i, j: (0, 0, i))
    per_tile = pl.BlockSpec((nheads, te // nkeys, tn), lambda i, j: (0, j, i))
    return pl.pallas_call(
        functools.partial(_peermain_kernel, nkeys=nkeys, nheads=nheads),
        out_shape=jax.ShapeDtypeStruct((d, n), F32),
        grid=(n // tn, ne // te),
        in_specs=[pl.BlockSpec((d, tn), lambda i, j: (0, i)),
                  pl.BlockSpec((te, d), lambda i, j: (j, 0)),
                  pl.BlockSpec((d, te), lambda i, j: (0, j)),
                  per_tile, per_tile, small, small],
        out_specs=pl.BlockSpec((d, tn), lambda i, j: (0, i)),
        scratch_shapes=[pltpu.VMEM((te, tn), F32), pltpu.VMEM((te, tn), BF16)],
        compiler_params=_params("parallel", "arbitrary"),
        name="peermain",
    )(h2t, u_bf16, vt_bf16, c1, a1, r2, a2)


def _ln2_kernel(x1_ref, fft_ref, g2_ref, lg_ref, lb_ref, y_ref, *, alpha):
    ff = fft_ref[...].T
    y_ref[...] = _layernorm_rows(alpha * x1_ref[...] + (1.0 + g2_ref[...]) * ff, lg_ref[...], lb_ref[...])


def _ln2(x1, fft, col0, ms, ln_g, ln_b, alpha):
    n, d = x1.shape
    tm = ms.tm
    c0 = col0 // tm
    return pl.pallas_call(
        functools.partial(_ln2_kernel, alpha=alpha),
        out_shape=jax.ShapeDtypeStruct((n, d), F32),
        grid=(n // tm,),
        in_specs=[pl.BlockSpec((tm, d), lambda i: (i, 0)),
                  pl.BlockSpec((d, tm), lambda i: (0, c0 + i)),
                  ms.spec(5, 1),
                  pl.BlockSpec((1, d), lambda i: (0, 0)), pl.BlockSpec((1, d), lambda i: (0, 0))],
        out_specs=pl.BlockSpec((tm, d), lambda i: (i, 0)),
        compiler_params=_params("parallel"),
        name="ln2",
    )(x1, fft, ms.arr, ln_g.reshape(1, d), ln_b.reshape(1, d))


def _pick(n, pref):
    for t in pref:
        if n % t == 0:
            return t
    raise ValueError(f"no tile for {n}")


def kernel(x_prompt, x_sample, cache_k, cache_v, state_hgrn, page_table, c_prompt, c_sample, w_ada, b_ada, w_in, lb_raw, norm_a, norm_b, sb_bias, w_out, ln1_g, ln1_b, ln2_g, ln2_b, peer_wq, peer_k1, peer_k2, peer_u, peer_v):
    depth = w_ada.shape[0]
    bp, tp, d = x_prompt.shape
    bs, ts, _ = x_sample.shape
    n_p, n_s = bp * tp, bs * ts
    page, hb = cache_k.shape[2], cache_k.shape[3]
    wb = hb * HEAD_DIM
    alpha = (2.0 * depth) ** 0.25
    lb_all = jnp.cumsum(jax.nn.softmax(lb_raw.astype(F32), axis=0), axis=0)

    tm_p = _pick(tp, (1024, 512, 256, 128))
    tm_s = _pick(n_s, (1024, 512, 256, 128))
    assert n_p % tm_s == 0
    n_all = n_p + n_s
    tn_k = _pick(n_all, (256, 128))
    tn_m = _pick(n_all, (512, 256, 128))
    tq = _pick(tp, (512, 256, 128))
    tk = min(tq, 256)

    cache_kt = jnp.transpose(cache_k, (0, 1, 3, 4, 2))
    cache_vt = jnp.transpose(cache_v, (0, 1, 3, 4, 2))
    yp = x_prompt.reshape(n_p, d)
    ys = x_sample.reshape(n_s, d)
    outs = {k: [] for k in ("kp", "vp", "sp", "ks", "vs", "ss")}
    for l in range(depth):
        mod = _mod(jnp.concatenate([c_prompt, c_sample], axis=0), w_ada[l], b_ada[l])
        ms_p = _ModSpec(mod[:bp].reshape(bp, 1, 6 * d), False, tm_p, tp // tm_p, d)
        ms_s = _ModSpec(jnp.repeat(mod[bp:], ts, axis=0), True, tm_s, 1, d)
        w_in_b = w_in[l].astype(BF16)
        w_out_b = w_out[l].astype(BF16)
        lb = lb_all[l]

        proj_p = _inproj(yp, ms_p, w_in_b, wb)
        proj_s = _inproj(ys, ms_s, w_in_b, wb)

        oa_p, sp = _hgrn(proj_p, lb, None, bp, tp)
        oa_s, ss = _hgrn(proj_s, lb, state_hgrn[l], bs, ts)

        ob_p = _sb_prompt(proj_p, sb_bias[l].astype(F32), bp, tp, tq, tk)
        q_hi = proj_s[4].reshape(bs, ts, hb, HEAD_DIM).transpose(0, 2, 1, 3).reshape(bs, hb * ts, HEAD_DIM)
        new_t = lambda a: jnp.pad(a.reshape(bs, ts, hb, HEAD_DIM).transpose(0, 2, 3, 1),
                                  ((0, 0), (0, 0), (0, 0), (0, page - ts)))
        ob_s = _sb_paged(q_hi, new_t(proj_s[5]), new_t(proj_s[6]), cache_kt, cache_vt, l, page_table, sb_bias[l])
        ob_s = ob_s.reshape(bs, hb, ts, HEAD_DIM).transpose(0, 2, 1, 3).reshape(n_s, wb)

        x1_p, h2t_p = _outproj(oa_p, proj_p, ob_p, yp, ms_p, norm_a[l], norm_b[l], w_out_b, ln1_g[l], ln1_b[l], alpha)
        x1_s, h2t_s = _outproj(oa_s, proj_s, ob_s, ys, ms_s, norm_a[l], norm_b[l], w_out_b, ln1_g[l], ln1_b[l], alpha)

        h2t = jnp.concatenate([h2t_p, h2t_s], axis=1)
        c1, a1, r2, a2 = _peer_topk(h2t, peer_wq[l].T.astype(BF16), peer_k1[l], peer_k2[l], tn_k)
        fft = _peer_main(h2t, peer_u[l].astype(BF16), peer_v[l].T.astype(BF16), c1, a1, r2, a2,
                         tn_m, 8 * peer_k1.shape[1])

        yp = _ln2(x1_p, fft, 0, ms_p, ln2_g[l], ln2_b[l], alpha)
        ys = _ln2(x1_s, fft, n_p, ms_s, ln2_g[l], ln2_b[l], alpha)

        outs["kp"].append(proj_p[5].reshape(bp, tp, hb, HEAD_DIM))
        outs["vp"].append(proj_p[6].reshape(bp, tp, hb, HEAD_DIM))
        outs["sp"].append(sp)
        outs["ks"].append(proj_s[5].reshape(bs, ts, hb, HEAD_DIM))
        outs["vs"].append(proj_s[6].reshape(bs, ts, hb, HEAD_DIM))
        outs["ss"].append(ss)
    return (yp.reshape(bp, tp, d), ys.reshape(bs, ts, d),
            jnp.stack(outs["kp"]), jnp.stack(outs["vp"]), jnp.stack(outs["sp"]),
            jnp.stack(outs["ks"]), jnp.stack(outs["vs"]), jnp.stack(outs["ss"]))
```

```python
import functools
import math

import numpy as np
import jax
import jax.numpy as jnp
from jax import lax
from jax.experimental import pallas as pl
from jax.experimental.pallas import tpu as pltpu

F32 = jnp.float32
BF16 = jnp.bfloat16

HEAD_DIM = 64
LANES = 128
LN_EPS = 1e-5
RMS_EPS = 1e-6
PEER_TOPK = 16
VMEM_LIMIT = 56 * 1024 * 1024

_NT = (((1,), (1,)), ((), ()))
_TN = (((0,), (0,)), ((), ()))


def _dot(a, b):
    return jnp.dot(a, b, preferred_element_type=F32)


def _dot_nt(a, b):
    return lax.dot_general(a, b, _NT, preferred_element_type=F32)


def _split3(x):
    hi = x.astype(BF16)
    r1 = x - hi.astype(F32)
    mid = r1.astype(BF16)
    lo = (r1 - mid.astype(F32)).astype(BF16)
    return hi, mid, lo


def _params(*sem):
    return pltpu.CompilerParams(dimension_semantics=sem, vmem_limit_bytes=VMEM_LIMIT)


def _mod_kernel(c_ref, w_ref, b_ref, o_ref):
    o_ref[...] = _dot(c_ref[...].astype(BF16), w_ref[...].astype(BF16)) + b_ref[...]


def _mod(c, w, b):
    n, d = c.shape
    width = w.shape[1]
    tn = 1024
    return pl.pallas_call(
        _mod_kernel,
        out_shape=jax.ShapeDtypeStruct((n, width), F32),
        grid=(width // tn,),
        in_specs=[pl.BlockSpec((n, d), lambda j: (0, 0)),
                  pl.BlockSpec((d, tn), lambda j: (0, j)),
                  pl.BlockSpec((1, tn), lambda j: (0, j))],
        out_specs=pl.BlockSpec((n, tn), lambda j: (0, j)),
        compiler_params=_params("parallel"),
        name="mod",
    )(c, w, b.reshape(1, width))


class _ModSpec:
    def __init__(self, arr, per_token, tm, tiles_per_seq, d):
        self.arr, self.per_token, self.tm, self.tps, self.d = arr, per_token, tm, tiles_per_seq, d

    def spec(self, part, ngrid):
        d, tps = self.d, self.tps
        if self.per_token:
            if ngrid == 1:
                return pl.BlockSpec((self.tm, d), lambda i: (i, part))
            return pl.BlockSpec((self.tm, d), lambda i, j: (i, part))
        if ngrid == 1:
            return pl.BlockSpec((None, 1, d), lambda i: (i // tps, 0, part))
        return pl.BlockSpec((None, 1, d), lambda i, j: (i // tps, 0, part))


def _inproj_kernel(x_ref, sh_ref, sc_ref, w_ref, o_ref, h_ref):
    @pl.when(pl.program_id(1) == 0)
    def _():
        h_ref[...] = (x_ref[...] * (1.0 + sc_ref[...]) + sh_ref[...]).astype(BF16)

    o_ref[...] = _dot(h_ref[...], w_ref[...])


def _inproj(x, ms, w_bf16, tn):
    n, d = x.shape
    tm = ms.tm
    nparts = w_bf16.shape[1] // tn
    return pl.pallas_call(
        _inproj_kernel,
        out_shape=jax.ShapeDtypeStruct((nparts, n, tn), F32),
        grid=(n // tm, nparts),
        in_specs=[pl.BlockSpec((tm, d), lambda i, j: (i, 0)),
                  ms.spec(0, 2), ms.spec(1, 2),
                  pl.BlockSpec((d, tn), lambda i, j: (0, j))],
        out_specs=pl.BlockSpec((None, tm, tn), lambda i, j: (j, i, 0)),
        scratch_shapes=[pltpu.VMEM((tm, d), BF16)],
        compiler_params=_params("parallel", "arbitrary"),
        name="inproj",
    )(x, ms.arr, ms.arr, w_bf16)


def _hgrn_consts(c, lseq):
    nlev = int(math.log2(lseq))
    r = np.arange(c)
    seg, pos = r // lseq, r % lseq
    j = np.arange(c)
    same = seg[:, None] == seg[None, :]
    mats = [same & (j[None, :] <= r[:, None]),
            same & (j[None, :] > r[:, None])]
    rq = np.zeros((nlev, c, 1), np.float32)
    mb = np.zeros((nlev + 1, c, c), np.float32)
    for lev in range(nlev):
        half, blk = 1 << lev, 2 << lev
        is_q = (pos % blk) >= half
        ref = (r // blk) * blk + half - 1
        m = np.where(is_q[:, None],
                     (j[None, :] > ref[:, None]) & (j[None, :] <= r[:, None]),
                     (j[None, :] > r[:, None]) & (j[None, :] <= ref[:, None]))
        mats.append(m)
        rq[lev, :, 0] = is_q
        mb[lev] = (r[:, None] // blk) == (r[None, :] // blk)
    mb[nlev] = np.eye(c)
    a = np.concatenate(mats, axis=0).astype(np.float32)
    rq = np.broadcast_to(rq, (nlev, c, LANES)).copy()
    mb2 = np.concatenate([mb, mb], axis=1)
    a2 = np.concatenate([a, a], axis=1)
    return jnp.asarray(a2, BF16), jnp.asarray(rq), jnp.asarray(mb2), nlev


def _hgrn_kernel(*refs, c, lseq, nch, nlev, has_s0, npp):
    if has_s0:
        q_ref, f_ref, i_ref, lb_ref, a_ref, rq_ref, mb_ref, s0_ref, o_ref, s_ref, st_ref = refs
    else:
        q_ref, f_ref, i_ref, lb_ref, a_ref, rq_ref, mb_ref, o_ref, s_ref, st_ref = refs
    nsq = c // lseq
    lane_lo = lax.broadcasted_iota(jnp.int32, (c, LANES), 1) < HEAD_DIM
    r128 = lax.broadcasted_iota(jnp.int32, (LANES, nsq * LANES), 0)
    c128 = lax.broadcasted_iota(jnp.int32, (LANES, nsq * LANES), 1) % LANES
    same_head = (r128 < HEAD_DIM) == (c128 < HEAD_DIM)
    row_seq = lax.broadcasted_iota(jnp.int32, (c, LANES), 0) // lseq
    zero64 = jnp.zeros((HEAD_DIM, HEAD_DIM), F32)

    if has_s0:
        for pp in range(npp):
            for s in range(nsq):
                top = jnp.concatenate([s0_ref[s, 2 * pp], zero64], axis=1)
                bot = jnp.concatenate([zero64, s0_ref[s, 2 * pp + 1]], axis=1)
                st_ref[pp, :, s * LANES:(s + 1) * LANES] = jnp.concatenate([top, bot], axis=0).T
    else:
        st_ref[...] = jnp.zeros_like(st_ref)

    def expand(x):
        if nsq == 1:
            return x
        return jnp.concatenate([jnp.where(row_seq == s, x, 0.0) for s in range(nsq)], axis=1)

    def heads_on_rows(x):
        return jnp.concatenate([jnp.where(lane_lo, x, 0.0), jnp.where(lane_lo, 0.0, x)], axis=0)

    def chunk(ci, carry):
        r0 = pl.multiple_of(ci * c, c)
        for pp in range(npp):
            ls = slice(pp * LANES, (pp + 1) * LANES)
            lbv = lb_ref[:, ls]
            qa = q_ref[pl.ds(r0, c), ls]
            z = f_ref[pl.ds(r0, c), ls]
            v = i_ref[pl.ds(r0, c), ls]
            lf = jnp.log(lbv + (1.0 - lbv) * jax.nn.sigmoid(z))
            kin = (1.0 - lbv) * jax.nn.sigmoid(-z)
            qs = qa * jax.nn.sigmoid(qa)
            hi = lf.astype(BF16)
            lo = (lf - hi.astype(F32)).astype(BF16)
            g = _dot(a_ref[...], jnp.concatenate([hi, lo], axis=0))
            eb = jnp.exp(g[0:c])
            qg = qs * eb
            kd = kin * jnp.exp(g[c:2 * c])
            st = st_ref[pp]
            o = _dot_nt(expand(qg).astype(BF16), st.astype(BF16))

            kin_b = kin.astype(BF16)
            sc = mb_ref[nlev] * _dot_nt(heads_on_rows(qs).astype(BF16), kin_b)
            for lev in range(nlev):
                y = jnp.exp(g[(2 + lev) * c:(3 + lev) * c])
                isq = rq_ref[lev]
                ql = qs * (y * isq)
                kl = kin * (y * (1.0 - isq))
                sc = sc + mb_ref[lev] * _dot_nt(heads_on_rows(ql).astype(BF16), kl.astype(BF16))
            r = _dot(sc.astype(BF16), v.astype(BF16))
            o_ref[pl.ds(r0, c), ls] = o + jnp.where(lane_lo, r[0:c], r[c:2 * c])

            if nsq == 1:
                dl = eb[c - 1:c, :]
            else:
                dl = jnp.concatenate([eb[(s + 1) * lseq - 1:(s + 1) * lseq, :] for s in range(nsq)], axis=1)
            u = _dot(v.T.astype(BF16), expand(kd).astype(BF16))
            st_ref[pp] = st * dl + jnp.where(same_head, u, 0.0)
        return carry

    lax.fori_loop(0, nch, chunk, 0)

    for pp in range(npp):
        for s in range(nsq):
            sbd = st_ref[pp, :, s * LANES:(s + 1) * LANES].T
            s_ref[s, 2 * pp] = sbd[0:HEAD_DIM, 0:HEAD_DIM]
            s_ref[s, 2 * pp + 1] = sbd[HEAD_DIM:, HEAD_DIM:]


def _hgrn(proj, lb, s0, nseq, t):
    _, n, width = proj.shape
    npair = width // LANES
    c = LANES
    lseq = min(t, c)
    nsq = c // lseq
    if t >= c:
        nch, rows, ngroups = t // c, t, nseq
    else:
        nch, rows, ngroups = 1, c, nseq // nsq
    npp = 4 if (nsq == 1 and npair % 4 == 0) else 1
    wl = npp * LANES
    a, rq, mb, nlev = _hgrn_consts(c, lseq)
    has_s0 = s0 is not None
    ins = [proj, proj, proj, lb.reshape(1, width), a, rq, mb]
    in_specs = [pl.BlockSpec((None, rows, wl), lambda g, p: (0, g, p)),
                pl.BlockSpec((None, rows, wl), lambda g, p: (1, g, p)),
                pl.BlockSpec((None, rows, wl), lambda g, p: (2, g, p)),
                pl.BlockSpec((1, wl), lambda g, p: (0, p)),
                pl.BlockSpec(a.shape, lambda g, p: (0, 0)),
                pl.BlockSpec(rq.shape, lambda g, p: (0, 0, 0)),
                pl.BlockSpec(mb.shape, lambda g, p: (0, 0, 0))]
    if has_s0:
        ins.append(s0)
        in_specs.append(pl.BlockSpec((nsq, 2 * npp, HEAD_DIM, HEAD_DIM), lambda g, p: (g, p, 0, 0)))
    kern = functools.partial(_hgrn_kernel, c=c, lseq=lseq, nch=nch, nlev=nlev, has_s0=has_s0, npp=npp)
    return pl.pallas_call(
        kern,
        out_shape=(jax.ShapeDtypeStruct((n, width), F32),
                   jax.ShapeDtypeStruct((nseq, 2 * npair, HEAD_DIM, HEAD_DIM), F32)),
        grid=(ngroups, npair // npp),
        in_specs=in_specs,
        out_specs=(pl.BlockSpec((rows, wl), lambda g, p: (g, p)),
                   pl.BlockSpec((nsq, 2 * npp, HEAD_DIM, HEAD_DIM), lambda g, p: (g, p, 0, 0))),
        scratch_shapes=[pltpu.VMEM((npp, LANES, nsq * LANES), F32)],
        compiler_params=_params("parallel", "parallel"),
        name="hgrn",
    )(*ins)


LOG2E = 1.4426950408889634


def _sb_block(z2, tail, uneg, mask):
    sp = jnp.maximum(z2, 0.0) + jnp.log2(1.0 + jnp.exp2(-jnp.abs(z2)))
    if mask is not None:
        sp = jnp.where(mask, sp, 0.0)
    rr = _dot(sp.astype(BF16), uneg)
    a = jnp.exp2(z2 - sp + (tail + rr))
    if mask is not None:
        a = jnp.where(mask, a, 0.0)
    return a, tail - jnp.sum(sp, axis=1, keepdims=True)


def _uneg_const(tk):
    j = np.arange(tk)
    return jnp.asarray(-(j[:, None] > j[None, :]).astype(np.float32), BF16)


def _sb_kernel(q_ref, k_ref, v_ref, bias_ref, u2_ref, o_ref, *, tq, tk):
    qi = pl.program_id(2)
    u2 = u2_ref[...]
    scale = HEAD_DIM ** -0.5 * LOG2E
    lane_lo = lax.broadcasted_iota(jnp.int32, (tq, LANES), 1) < HEAD_DIM
    q = q_ref[...] * scale
    ndiag = tq // tk
    base = qi * ndiag
    qh = (jnp.where(lane_lo, q, 0.0).astype(BF16), jnp.where(lane_lo, 0.0, q).astype(BF16))
    bias = (bias_ref[2 * pl.program_id(1)] * LOG2E, bias_ref[2 * pl.program_id(1) + 1] * LOG2E)
    below_diag = (lax.broadcasted_iota(jnp.int32, (tk, tk), 1) < lax.broadcasted_iota(jnp.int32, (tk, tk), 0))

    def logits(qpair, kb):
        kk = k_ref[pl.ds(pl.multiple_of(kb * tk, tk), tk), :].astype(BF16)
        return tuple(_dot_nt(qpair[h], kk) + bias[h] for h in range(2))

    def finish(z, kb, carry, mask):
        vv = v_ref[pl.ds(pl.multiple_of(kb * tk, tk), tk), :].astype(BF16)
        out = []
        for h in range(2):
            acc, tail = carry[h]
            a, tail = _sb_block(z[h], tail, u2, mask)
            out.append((acc + _dot(a.astype(BF16), vv), tail))
        return tuple(out)

    bands = []
    for r in range(ndiag):
        qband = tuple(x[r * tk:(r + 1) * tk] for x in qh)
        zero = (jnp.zeros((tk, LANES), F32), jnp.zeros((tk, 1), F32))
        c = finish(logits(qband, base + r), base + r, (zero, zero), below_diag)
        for kb in range(r - 1, -1, -1):
            c = finish(logits(qband, base + kb), base + kb, c, None)
        bands.append(c)
    carry = tuple(tuple(jnp.concatenate([b[h][j] for b in bands], axis=0) for j in range(2)) for h in range(2))

    def body(i, state):
        c, z = state
        kb = base - 1 - i
        z_next = logits(qh, jnp.maximum(kb - 1, 0))
        return finish(z, kb, c, None), z_next

    carry, _ = lax.fori_loop(0, base, body, (carry, logits(qh, jnp.maximum(base - 1, 0))))
    o_ref[...] = jnp.where(lane_lo, carry[0][0], carry[1][0])


def _sb_prompt(proj, bias, nseq, t, tq, tk):
    _, n, width = proj.shape
    npair = width // LANES
    nq = t // tq
    u2 = _uneg_const(tk)
    kern = functools.partial(_sb_kernel, tq=tq, tk=tk)
    return pl.pallas_call(
        kern,
        out_shape=jax.ShapeDtypeStruct((n, width), F32),
        grid=(nseq, npair, nq),
        in_specs=[pl.BlockSpec((None, tq, LANES), lambda b, p, i: (4, b * nq + i, p)),
                  pl.BlockSpec((None, t, LANES), lambda b, p, i: (5, b, p)),
                  pl.BlockSpec((None, t, LANES), lambda b, p, i: (6, b, p)),
                  pl.BlockSpec(memory_space=pltpu.SMEM),
                  pl.BlockSpec(u2.shape, lambda b, p, i: (0, 0))],
        out_specs=pl.BlockSpec((tq, LANES), lambda b, p, i: (b * nq + i, p)),
        compiler_params=_params("parallel", "parallel", "arbitrary"),
        name="sb",
    )(proj, proj, proj, bias, u2)


def _sbpaged_kernel(pt_ref, q_ref, kn_ref, vn_ref, bias_ref, u2_ref, *rest, npages, page, t, nh):
    k_refs = rest[:npages]
    v_refs = rest[npages:2 * npages]
    o_ref = rest[2 * npages]
    rows = nh * t
    scale = HEAD_DIM ** -0.5 * LOG2E
    u2 = u2_ref[...]
    qa = (q_ref[...] * scale).astype(BF16)
    bias = bias_ref[...]
    head_z = lax.broadcasted_iota(jnp.int32, (rows, page), 0) // t
    head_o = lax.broadcasted_iota(jnp.int32, (rows, HEAD_DIM), 0) // t

    order = list(range(npages - 1, -1, -1))
    kblk = [kn_ref] + [k_refs[p] for p in order]
    vblk = [vn_ref] + [v_refs[p] for p in order]
    nblk = npages + 1

    def logits(x_ref):
        z = bias
        for h in range(nh):
            z = z + jnp.where(head_z == h, _dot(qa, x_ref[h].astype(BF16)), 0.0)
        return z

    z2 = jnp.concatenate([logits(x) for x in kblk], axis=0)
    qpos = lax.broadcasted_iota(jnp.int32, (rows, page), 0) % t
    kpos = lax.broadcasted_iota(jnp.int32, (rows, page), 1)
    visible = jnp.concatenate([kpos < qpos] + [jnp.ones((rows, page), jnp.bool_)] * npages, axis=0)
    sp = jnp.where(visible, jnp.maximum(z2, 0.0) + jnp.log2(1.0 + jnp.exp2(-jnp.abs(z2))), 0.0)
    rr = _dot(sp.astype(BF16), u2)
    blk_sum = jnp.sum(sp, axis=1, keepdims=True)
    tails = [jnp.zeros((rows, 1), F32)]
    for b in range(nblk - 1):
        tails.append(tails[-1] - blk_sum[b * rows:(b + 1) * rows])
    a = jnp.where(visible, jnp.exp2(z2 - sp + (jnp.concatenate(tails, axis=0) + rr)), 0.0).astype(BF16)
    acc = jnp.zeros((rows, HEAD_DIM), F32)
    for b in range(nblk):
        a_b = a[b * rows:(b + 1) * rows]
        for h in range(nh):
            acc = acc + jnp.where(head_o == h, _dot_nt(a_b, vblk[b][h].astype(BF16)), 0.0)
    o_ref[...] = acc


def _sb_paged(q, kn_t, vn_t, cache_kt, cache_vt, layer, page_table, sb_bias):
    nseq, rows, _ = q.shape
    nh, page = cache_kt.shape[2], cache_kt.shape[4]
    t = rows // nh
    npages = page_table.shape[1]
    u2 = _uneg_const(page)
    bias = jnp.broadcast_to(jnp.repeat(sb_bias.astype(F32) * LOG2E, t)[:, None], (rows, page))
    kern = functools.partial(_sbpaged_kernel, npages=npages, page=page, t=t, nh=nh)

    def page_spec(p):
        return pl.BlockSpec((None, None, nh, HEAD_DIM, page), lambda b, pt: (layer, pt[b, p], 0, 0, 0))

    new_spec = pl.BlockSpec((None, nh, HEAD_DIM, page), lambda b, pt: (b, 0, 0, 0))
    grid_spec = pltpu.PrefetchScalarGridSpec(
        num_scalar_prefetch=1,
        grid=(nseq,),
        in_specs=[pl.BlockSpec((None, rows, HEAD_DIM), lambda b, pt: (b, 0, 0)),
                  new_spec, new_spec,
                  pl.BlockSpec((rows, page), lambda b, pt: (0, 0)),
                  pl.BlockSpec(u2.shape, lambda b, pt: (0, 0))]
                 + [page_spec(p) for p in range(npages)] * 2,
        out_specs=pl.BlockSpec((None, rows, HEAD_DIM), lambda b, pt: (b, 0, 0)),
    )
    return pl.pallas_call(
        kern,
        out_shape=jax.ShapeDtypeStruct((nseq, rows, HEAD_DIM), F32),
        grid_spec=grid_spec,
        compiler_params=_params("arbitrary"),
        name="sbpaged",
    )(page_table, q, kn_t, vn_t, bias, u2, *([cache_kt] * npages), *([cache_vt] * npages))


def _layernorm_rows(x, g, b):
    mu = jnp.mean(x, axis=-1, keepdims=True)
    xc = x - mu
    var = jnp.mean(xc * xc, axis=-1, keepdims=True)
    return xc * lax.rsqrt(var + LN_EPS) * g + b


def _head_rms(o, hsum):
    sq = o * o
    hi = sq.astype(BF16)
    lo = (sq - hi.astype(F32)).astype(BF16)
    ms = (_dot(hi, hsum) + _dot(lo, hsum)) * (1.0 / HEAD_DIM)
    return o * lax.rsqrt(ms + RMS_EPS)


def _outproj_kernel(oa_ref, ga_ref, ob_ref, x_ref, g1_ref, sh2_ref, sc2_ref, na_ref, nb_ref, hs_ref,
                    w_ref, lg_ref, lb_ref, x1_ref, h2t_ref, *, alpha):
    hs = hs_ref[...]
    ga = ga_ref[...]
    ya = _head_rms(oa_ref[...], hs) * na_ref[...] * (ga * jax.nn.sigmoid(ga))
    yb = _head_rms(ob_ref[...], hs) * nb_ref[...]
    wa = w_ref.shape[0] // 2
    mix = _dot(ya.astype(BF16), w_ref[0:wa, :]) + _dot(yb.astype(BF16), w_ref[wa:, :])
    x1 = _layernorm_rows(alpha * x_ref[...] + (1.0 + g1_ref[...]) * mix, lg_ref[...], lb_ref[...])
    x1_ref[...] = x1
    h2 = x1 * (1.0 + sc2_ref[...]) + sh2_ref[...]
    h2t_ref[...] = h2.T.astype(BF16)


def _outproj(oa, proj, ob, x, ms, norm_a, norm_b, w_out_bf16, ln_g, ln_b, alpha):
    n, d = x.shape
    tm = ms.tm
    wa = oa.shape[1]
    hs = (np.arange(wa)[:, None] // HEAD_DIM == np.arange(wa)[None, :] // HEAD_DIM).astype(np.float32)
    row = lambda i: (i, 0)
    fixed = lambda i: (0, 0)
    return pl.pallas_call(
        functools.partial(_outproj_kernel, alpha=alpha),
        out_shape=(jax.ShapeDtypeStruct((n, d), F32), jax.ShapeDtypeStruct((d, n), BF16)),
        grid=(n // tm,),
        in_specs=[pl.BlockSpec((tm, wa), row),
                  pl.BlockSpec((None, tm, wa), lambda i: (3, i, 0)),
                  pl.BlockSpec((tm, wa), row),
                  pl.BlockSpec((tm, d), row),
                  ms.spec(2, 1), ms.spec(3, 1), ms.spec(4, 1),
                  pl.BlockSpec((1, wa), fixed), pl.BlockSpec((1, wa), fixed),
                  pl.BlockSpec((wa, wa), fixed),
                  pl.BlockSpec((2 * wa, d), fixed),
                  pl.BlockSpec((1, d), fixed), pl.BlockSpec((1, d), fixed)],
        out_specs=(pl.BlockSpec((tm, d), row), pl.BlockSpec((d, tm), lambda i: (0, i))),
        compiler_params=_params("parallel"),
        name="outproj",
    )(oa, proj, ob, x, ms.arr, ms.arr, ms.arr, norm_a.reshape(1, wa), norm_b.reshape(1, wa),
      jnp.asarray(hs, BF16), w_out_bf16, ln_g.reshape(1, d), ln_b.reshape(1, d))


MARK0 = -(2.0 ** 127)
MSTEP = 2.0 ** 105
CAND_ROWS = 64


def _stair_cells():
    k = PEER_TOPK
    return [(a, b) for a in range(k) for b in range(k) if (a + 1) * (b + 1) <= k]


def _cand_consts():
    cells = _stair_cells()
    g1 = np.zeros((CAND_ROWS, PEER_TOPK), np.float32)
    g2 = np.zeros((CAND_ROWS, PEER_TOPK), np.float32)
    for r, (a, b) in enumerate(cells):
        g1[r, a] = 1.0
        g2[r, b] = 1.0
    return jnp.asarray(g1, BF16), jnp.asarray(g2, BF16), jnp.asarray(g1.T.copy(), BF16), len(cells)


def _extract_topk(s, k):
    rows, tn = s.shape
    ridx = lax.broadcasted_iota(jnp.int32, (rows, tn), 0)
    kidx = lax.broadcasted_iota(jnp.int32, (k, tn), 0)

    def body(i, carry):
        cur, vals, rank = carry
        m = jnp.max(cur, axis=0, keepdims=True)
        first = jnp.min(jnp.where(cur == m, ridx, rows), axis=0, keepdims=True)
        hit = ridx == first
        cur = jnp.where(hit, -jnp.inf, cur)
        rank = jnp.where(hit, jnp.full((1, tn), i, jnp.int32).astype(F32), rank)
        vals = jnp.where(kidx == i, m, vals)
        return cur, vals, rank

    init = (s, jnp.zeros((k, tn), F32), jnp.full((rows, tn), float(k), F32))
    _, vals, rank = lax.fori_loop(0, k, body, init)
    return vals, rank


def _knock_topk(cur_ref, val_ref, narr, k):
    tn = cur_ref.shape[2]

    def body(i, carry):
        mark = MARK0 - jnp.full((1, tn), i, jnp.int32).astype(F32) * MSTEP
        for a in range(narr):
            cur = cur_ref[a]
            m = jnp.max(cur, axis=0, keepdims=True)
            cur_ref[a] = jnp.where(cur == m, mark, cur)
            val_ref[a, pl.ds(i, 1), :] = m
        return carry

    lax.fori_loop(0, k, body, 0)


def _tie_free(cur_ref, narr, k, valid=None):
    ok = None
    for a in range(narr):
        gone = cur_ref[a] <= MARK0
        if valid is not None:
            gone = gone & valid
        good = jnp.sum(gone.astype(F32), axis=0, keepdims=True) == float(k)
        ok = good if ok is None else (ok & good)
    return jnp.min(ok.astype(jnp.int32)) == 1


def _exact_topk_into(src_ref, cur_ref, val_ref, narr, k):
    for a in range(narr):
        s = src_ref[a]
        vals, rank = _extract_topk(s, k)
        val_ref[a] = vals
        cur_ref[a] = jnp.where(rank < float(k), MARK0 - rank * MSTEP, s)


def _bcast_rows_bf16(row, nrows):
    tile = jnp.broadcast_to(row, (16, row.shape[1])).astype(BF16)
    return jnp.concatenate([tile] * (nrows // 16), axis=0)


def _peertopk_kernel(ht_ref, wqt_ref, k1_ref, k2_ref, g1_ref, g2_ref, ga_ref, c1_ref, a1_ref, r2_ref, a2_ref,
                     s_scr, cur_scr, val_scr, cand_scr, ccur_scr, cval_scr, *, nheads, nkeys, ncells):
    tn = ht_ref.shape[1]
    dq = k1_ref.shape[1]
    k = PEER_TOPK
    qt = _dot(wqt_ref[...], ht_ref[...]).astype(BF16)
    k1 = k1_ref[...].astype(BF16)
    k2 = k2_ref[...].astype(BF16)
    for h in range(nheads):
        s_scr[2 * h] = _dot(k1, qt[(2 * h) * dq:(2 * h + 1) * dq, :])
        s_scr[2 * h + 1] = _dot(k2, qt[(2 * h + 1) * dq:(2 * h + 2) * dq, :])
    cur_scr[...] = s_scr[...]
    _knock_topk(cur_scr, val_scr, 2 * nheads, k)

    @pl.when(jnp.logical_not(_tie_free(cur_scr, 2 * nheads, k)))
    def _():
        _exact_topk_into(s_scr, cur_scr, val_scr, 2 * nheads, k)

    g1, g2 = g1_ref[...], g2_ref[...]
    valid = lax.broadcasted_iota(jnp.int32, (CAND_ROWS, tn), 0) < ncells

    def pick(g, v):
        hi, mid, lo = _split3(v)
        return _dot(g, hi) + _dot(g, mid) + _dot(g, lo)

    for h in range(nheads):
        cand = jnp.where(valid, pick(g1, val_scr[2 * h]) + pick(g2, val_scr[2 * h + 1]), -jnp.inf)
        cand_scr[h] = cand
        ccur_scr[h] = cand
    _knock_topk(ccur_scr, cval_scr, nheads, k)

    @pl.when(jnp.logical_not(_tie_free(ccur_scr, nheads, k, valid)))
    def _():
        _exact_topk_into(cand_scr, ccur_scr, cval_scr, nheads, k)

    ga = ga_ref[...]
    inv_step = 1.0 / MSTEP
    for h in range(nheads):
        taken = ((ccur_scr[h] <= MARK0) & valid).astype(BF16)
        cnt = _dot(ga, taken)
        top = cval_scr[h]
        inv_z = 1.0 / jnp.sum(jnp.exp(top - top[0:1]), axis=0, keepdims=True)
        s1, s2 = s_scr[2 * h], s_scr[2 * h + 1]
        cur1, cur2 = cur_scr[2 * h], cur_scr[2 * h + 1]
        sel1, sel2 = cur1 <= MARK0, cur2 <= MARK0
        rank1 = jnp.where(sel1, (MARK0 - cur1) * inv_step, float(k)).astype(BF16)
        rank2 = jnp.where(sel2, (MARK0 - cur2) * inv_step, float(k))
        c1 = jnp.zeros((nkeys, tn), BF16)
        for a in range(k):
            c1 = jnp.where(rank1 == a, _bcast_rows_bf16(cnt[a:a + 1], nkeys), c1)
        v1 = val_scr[2 * h]
        v2 = val_scr[2 * h + 1]
        c1_ref[h] = c1.astype(F32)
        a1_ref[h] = jnp.where(sel1, jnp.exp(s1 - v1[0:1]) * inv_z, 0.0)
        r2_ref[h] = rank2.astype(BF16)
        a2_ref[h] = jnp.where(sel2, jnp.exp(s2 - v2[0:1]), 0.0).astype(BF16)


def _peer_topk(h2t, wqt_bf16, k1, k2, tn):
    d, n = h2t.shape
    nkeys, dq = k1.shape
    nheads = wqt_bf16.shape[0] // (2 * dq)
    g1, g2, ga, ncells = _cand_consts()
    shp = (nheads, nkeys, n)
    blk = pl.BlockSpec((nheads, nkeys, tn), lambda i: (0, 0, i))
    fixed = lambda i: (0, 0)
    k = PEER_TOPK
    return pl.pallas_call(
        functools.partial(_peertopk_kernel, nheads=nheads, nkeys=nkeys, ncells=ncells),
        out_shape=(jax.ShapeDtypeStruct(shp, F32), jax.ShapeDtypeStruct(shp, F32),
                   jax.ShapeDtypeStruct(shp, BF16), jax.ShapeDtypeStruct(shp, BF16)),
        grid=(n // tn,),
        in_specs=[pl.BlockSpec((d, tn), lambda i: (0, i)),
                  pl.BlockSpec(wqt_bf16.shape, fixed),
                  pl.BlockSpec((nkeys, dq), fixed), pl.BlockSpec((nkeys, dq), fixed),
                  pl.BlockSpec(g1.shape, fixed), pl.BlockSpec(g2.shape, fixed), pl.BlockSpec(ga.shape, fixed)],
        out_specs=(blk, blk, blk, blk),
        scratch_shapes=[pltpu.VMEM((2 * nheads, nkeys, tn), F32), pltpu.VMEM((2 * nheads, nkeys, tn), F32),
                        pltpu.VMEM((2 * nheads, k, tn), F32),
                        pltpu.VMEM((nheads, CAND_ROWS, tn), F32), pltpu.VMEM((nheads, CAND_ROWS, tn), F32),
                        pltpu.VMEM((nheads, k, tn), F32)],
        compiler_params=_params("parallel"),
        name="peertopk",
    )(h2t, wqt_bf16, k1, k2, g1, g2, ga)


GATE_LANES = 256


def _gelu_tanh(x):
    c = 0.7978845608028654
    hx = 0.5 * x
    return hx + hx * jnp.tanh(x * (c + (c * 0.044715) * (x * x)))


def _peermain_kernel(ht_ref, u_ref, vt_ref, c1_ref, a1_ref, r2_ref, a2_ref, o_ref, act_ref, g_ref, *, nkeys, nheads):
    j = pl.program_id(1)
    te, tn = act_ref.shape
    gl = min(GATE_LANES, tn)
    assert tn % gl == 0
    zero = jnp.zeros((), BF16)

    @pl.when(j == 0)
    def _():
        o_ref[...] = jnp.zeros_like(o_ref)

    act_ref[...] = _dot(u_ref[...], ht_ref[...])
    for sub in range(te // nkeys):
        arow = slice(sub * nkeys, (sub + 1) * nkeys)
        for lt in range(tn // gl):
            cols = slice(lt * gl, (lt + 1) * gl)
            w = jnp.zeros((nkeys, gl), BF16)
            for h in range(nheads):
                c1 = _bcast_rows_bf16(c1_ref[h, sub:sub + 1, cols], nkeys)
                a1 = _bcast_rows_bf16(a1_ref[h, sub:sub + 1, cols], nkeys)
                w = w + a1 * jnp.where(r2_ref[h, :, cols] < c1, a2_ref[h, :, cols], zero)
            g_ref[arow, cols] = w * _gelu_tanh(act_ref[arow, cols]).astype(BF16)
    o_ref[...] += _dot(vt_ref[...], g_ref[...])


def _peer_main(h2t, u_bf16, vt_bf16, c1, a1, r2, a2, tn, te):
    d, n = h2t.shape
    ne = u_bf16.shape[0]
    nheads, nkeys, _ = c1.shape
    small = pl.BlockSpec((nheads, nkeys, tn), lambda i, j: (0, 0, i))
    per_tile = pl.BlockSpec((nheads, te // nkeys, tn), lambda i, j: (0, j, i))
    return pl.pallas_call(
        functools.partial(_peermain_kernel, nkeys=nkeys, nheads=nheads),
        out_shape=jax.ShapeDtypeStruct((d, n), F32),
        grid=(n // tn, ne // te),
        in_specs=[pl.BlockSpec((d, tn), lambda i, j: (0, i)),
                  pl.BlockSpec((te, d), lambda i, j: (j, 0)),
                  pl.BlockSpec((d, te), lambda i, j: (0, j)),
                  per_tile, per_tile, small, small],
        out_specs=pl.BlockSpec((d, tn), lambda i, j: (0, i)),
        scratch_shapes=[pltpu.VMEM((te, tn), F32), pltpu.VMEM((te, tn), BF16)],
        compiler_params=_params("parallel", "arbitrary"),
        name="peermain",
    )(h2t, u_bf16, vt_bf16, c1, a1, r2, a2)


def _ln2_kernel(x1_ref, fft_ref, g2_ref, lg_ref, lb_ref, y_ref, *, alpha):
    ff = fft_ref[...].T
    y_ref[...] = _layernorm_rows(alpha * x1_ref[...] + (1.0 + g2_ref[...]) * ff, lg_ref[...], lb_ref[...])


def _ln2(x1, fft, col0, ms, ln_g, ln_b, alpha):
    n, d = x1.shape
    tm = ms.tm
    c0 = col0 // tm
    return pl.pallas_call(
        functools.partial(_ln2_kernel, alpha=alpha),
        out_shape=jax.ShapeDtypeStruct((n, d), F32),
        grid=(n // tm,),
        in_specs=[pl.BlockSpec((tm, d), lambda i: (i, 0)),
                  pl.BlockSpec((d, tm), lambda i: (0, c0 + i)),
                  ms.spec(5, 1),
                  pl.BlockSpec((1, d), lambda i: (0, 0)), pl.BlockSpec((1, d), lambda i: (0, 0))],
        out_specs=pl.BlockSpec((tm, d), lambda i: (i, 0)),
        compiler_params=_params("parallel"),
        name="ln2",
    )(x1, fft, ms.arr, ln_g.reshape(1, d), ln_b.reshape(1, d))


def _pick(n, pref):
    for t in pref:
        if n % t == 0:
            return t
    raise ValueError(f"no tile for {n}")


def kernel(x_prompt, x_sample, cache_k, cache_v, state_hgrn, page_table, c_prompt, c_sample, w_ada, b_ada, w_in, lb_raw, norm_a, norm_b, sb_bias, w_out, ln1_g, ln1_b, ln2_g, ln2_b, peer_wq, peer_k1, peer_k2, peer_u, peer_v):
    depth = w_ada.shape[0]
    bp, tp, d = x_prompt.shape
    bs, ts, _ = x_sample.shape
    n_p, n_s = bp * tp, bs * ts
    page, hb = cache_k.shape[2], cache_k.shape[3]
    wb = hb * HEAD_DIM
    alpha = (2.0 * depth) ** 0.25
    lb_all = jnp.cumsum(jax.nn.softmax(lb_raw.astype(F32), axis=0), axis=0)

    tm_p = _pick(tp, (1024, 512, 256, 128))
    tm_s = _pick(n_s, (1024, 512, 256, 128))
    assert n_p % tm_s == 0
    n_all = n_p + n_s
    tn_k = _pick(n_all, (256, 128))
    tn_m = _pick(n_all, (512, 256, 128))
    tq = _pick(tp, (512, 256, 128))
    tk = min(tq, 256)

    cache_kt = jnp.transpose(cache_k, (0, 1, 3, 4, 2))
    cache_vt = jnp.transpose(cache_v, (0, 1, 3, 4, 2))
    yp = x_prompt.reshape(n_p, d)
    ys = x_sample.reshape(n_s, d)
    outs = {k: [] for k in ("kp", "vp", "sp", "ks", "vs", "ss")}
    for l in range(depth):
        mod = _mod(jnp.concatenate([c_prompt, c_sample], axis=0), w_ada[l], b_ada[l])
        ms_p = _ModSpec(mod[:bp].reshape(bp, 1, 6 * d), False, tm_p, tp // tm_p, d)
        ms_s = _ModSpec(jnp.repeat(mod[bp:], ts, axis=0), True, tm_s, 1, d)
        w_in_b = w_in[l].astype(BF16)
        w_out_b = w_out[l].astype(BF16)
        lb = lb_all[l]

        proj_p = _inproj(yp, ms_p, w_in_b, wb)
        proj_s = _inproj(ys, ms_s, w_in_b, wb)

        oa_p, sp = _hgrn(proj_p, lb, None, bp, tp)
        oa_s, ss = _hgrn(proj_s, lb, state_hgrn[l], bs, ts)

        ob_p = _sb_prompt(proj_p, sb_bias[l].astype(F32), bp, tp, tq, tk)
        q_hi = proj_s[4].reshape(bs, ts, hb, HEAD_DIM).transpose(0, 2, 1, 3).reshape(bs, hb * ts, HEAD_DIM)
        new_t = lambda a: jnp.pad(a.reshape(bs, ts, hb, HEAD_DIM).transpose(0, 2, 3, 1),
                                  ((0, 0), (0, 0), (0, 0), (0, page - ts)))
        ob_s = _sb_paged(q_hi, new_t(proj_s[5]), new_t(proj_s[6]), cache_kt, cache_vt, l, page_table, sb_bias[l])
        ob_s = ob_s.reshape(bs, hb, ts, HEAD_DIM).transpose(0, 2, 1, 3).reshape(n_s, wb)

        x1_p, h2t_p = _outproj(oa_p, proj_p, ob_p, yp, ms_p, norm_a[l], norm_b[l], w_out_b, ln1_g[l], ln1_b[l], alpha)
        x1_s, h2t_s = _outproj(oa_s, proj_s, ob_s, ys, ms_s, norm_a[l], norm_b[l], w_out_b, ln1_g[l], ln1_b[l], alpha)

        h2t = jnp.concatenate([h2t_p, h2t_s], axis=1)
        c1, a1, r2, a2 = _peer_topk(h2t, peer_wq[l].T.astype(BF16), peer_k1[l], peer_k2[l], tn_k)
        fft = _peer_main(h2t, peer_u[l].astype(BF16), peer_v[l].T.astype(BF16), c1, a1, r2, a2,
                         tn_m, 16 * peer_k1.shape[1])

        yp = _ln2(x1_p, fft, 0, ms_p, ln2_g[l], ln2_b[l], alpha)
        ys = _ln2(x1_s, fft, n_p, ms_s, ln2_g[l], ln2_b[l], alpha)

        outs["kp"].append(proj_p[5].reshape(bp, tp, hb, HEAD_DIM))
        outs["vp"].append(proj_p[6].reshape(bp, tp, hb, HEAD_DIM))
        outs["sp"].append(sp)
        outs["ks"].append(proj_s[5].reshape(bs, ts, hb, HEAD_DIM))
        outs["vs"].append(proj_s[6].reshape(bs, ts, hb, HEAD_DIM))
        outs["ss"].append(ss)
    return (yp.reshape(bp, tp, d), ys.reshape(bs, ts, d),
            jnp.stack(outs["kp"]), jnp.stack(outs["vp"]), jnp.stack(outs["sp"]),
            jnp.stack(outs["ks"]), jnp.stack(outs["vs"]), jnp.stack(outs["ss"]))
```

```python
import functools
import math

import numpy as np
import jax
import jax.numpy as jnp
from jax import lax
from jax.experimental import pallas as pl
from jax.experimental.pallas import tpu as pltpu

F32 = jnp.float32
BF16 = jnp.bfloat16

HEAD_DIM = 64
LANES = 128
LN_EPS = 1e-5
RMS_EPS = 1e-6
PEER_TOPK = 16
VMEM_LIMIT = 56 * 1024 * 1024

_NT = (((1,), (1,)), ((), ()))
_TN = (((0,), (0,)), ((), ()))


def _dot(a, b):
    return jnp.dot(a, b, preferred_element_type=F32)


def _dot_nt(a, b):
    return lax.dot_general(a, b, _NT, preferred_element_type=F32)


def _split3(x):
    hi = x.astype(BF16)
    r1 = x - hi.astype(F32)
    mid = r1.astype(BF16)
    lo = (r1 - mid.astype(F32)).astype(BF16)
    return hi, mid, lo


def _params(*sem):
    return pltpu.CompilerParams(dimension_semantics=sem, vmem_limit_bytes=VMEM_LIMIT)


def _mod_kernel(c_ref, w_ref, b_ref, o_ref):
    o_ref[...] = _dot(c_ref[...].astype(BF16), w_ref[...].astype(BF16)) + b_ref[...]


def _mod(c, w, b):
    n, d = c.shape
    width = w.shape[1]
    tn = 1024
    return pl.pallas_call(
        _mod_kernel,
        out_shape=jax.ShapeDtypeStruct((n, width), F32),
        grid=(width // tn,),
        in_specs=[pl.BlockSpec((n, d), lambda j: (0, 0)),
                  pl.BlockSpec((d, tn), lambda j: (0, j)),
                  pl.BlockSpec((1, tn), lambda j: (0, j))],
        out_specs=pl.BlockSpec((n, tn), lambda j: (0, j)),
        compiler_params=_params("parallel"),
        name="mod",
    )(c, w, b.reshape(1, width))


class _ModSpec:
    def __init__(self, arr, per_token, tm, tiles_per_seq, d):
        self.arr, self.per_token, self.tm, self.tps, self.d = arr, per_token, tm, tiles_per_seq, d

    def spec(self, part, ngrid):
        d, tps = self.d, self.tps
        if self.per_token:
            if ngrid == 1:
                return pl.BlockSpec((self.tm, d), lambda i: (i, part))
            return pl.BlockSpec((self.tm, d), lambda i, j: (i, part))
        if ngrid == 1:
            return pl.BlockSpec((None, 1, d), lambda i: (i // tps, 0, part))
        return pl.BlockSpec((None, 1, d), lambda i, j: (i // tps, 0, part))


def _inproj_kernel(x_ref, sh_ref, sc_ref, w_ref, o_ref, h_ref):
    @pl.when(pl.program_id(1) == 0)
    def _():
        h_ref[...] = (x_ref[...] * (1.0 + sc_ref[...]) + sh_ref[...]).astype(BF16)

    o_ref[...] = _dot(h_ref[...], w_ref[...])


def _inproj(x, ms, w_bf16, tn):
    n, d = x.shape
    tm = ms.tm
    nparts = w_bf16.shape[1] // tn
    return pl.pallas_call(
        _inproj_kernel,
        out_shape=jax.ShapeDtypeStruct((nparts, n, tn), F32),
        grid=(n // tm, nparts),
        in_specs=[pl.BlockSpec((tm, d), lambda i, j: (i, 0)),
                  ms.spec(0, 2), ms.spec(1, 2),
                  pl.BlockSpec((d, tn), lambda i, j: (0, j))],
        out_specs=pl.BlockSpec((None, tm, tn), lambda i, j: (j, i, 0)),
        scratch_shapes=[pltpu.VMEM((tm, d), BF16)],
        compiler_params=_params("parallel", "arbitrary"),
        name="inproj",
    )(x, ms.arr, ms.arr, w_bf16)


def _hgrn_consts(c, lseq):
    nlev = int(math.log2(lseq))
    r = np.arange(c)
    seg, pos = r // lseq, r % lseq
    j = np.arange(c)
    same = seg[:, None] == seg[None, :]
    mats = [same & (j[None, :] <= r[:, None]),
            same & (j[None, :] > r[:, None])]
    rq = np.zeros((nlev, c, 1), np.float32)
    mb = np.zeros((nlev + 1, c, c), np.float32)
    for lev in range(nlev):
        half, blk = 1 << lev, 2 << lev
        is_q = (pos % blk) >= half
        ref = (r // blk) * blk + half - 1
        m = np.where(is_q[:, None],
                     (j[None, :] > ref[:, None]) & (j[None, :] <= r[:, None]),
                     (j[None, :] > r[:, None]) & (j[None, :] <= ref[:, None]))
        mats.append(m)
        rq[lev, :, 0] = is_q
        mb[lev] = (r[:, None] // blk) == (r[None, :] // blk)
    mb[nlev] = np.eye(c)
    a = np.concatenate(mats, axis=0).astype(np.float32)
    rq = np.broadcast_to(rq, (nlev, c, LANES)).copy()
    mb2 = np.concatenate([mb, mb], axis=1)
    a2 = np.concatenate([a, a], axis=1)
    return jnp.asarray(a2, BF16), jnp.asarray(rq), jnp.asarray(mb2), nlev


def _hgrn_kernel(*refs, c, lseq, nch, nlev, has_s0, npp):
    if has_s0:
        q_ref, f_ref, i_ref, lb_ref, a_ref, rq_ref, mb_ref, s0_ref, o_ref, s_ref, st_ref = refs
    else:
        q_ref, f_ref, i_ref, lb_ref, a_ref, rq_ref, mb_ref, o_ref, s_ref, st_ref = refs
    nsq = c // lseq
    lane_lo = lax.broadcasted_iota(jnp.int32, (c, LANES), 1) < HEAD_DIM
    r128 = lax.broadcasted_iota(jnp.int32, (LANES, nsq * LANES), 0)
    c128 = lax.broadcasted_iota(jnp.int32, (LANES, nsq * LANES), 1) % LANES
    same_head = (r128 < HEAD_DIM) == (c128 < HEAD_DIM)
    row_seq = lax.broadcasted_iota(jnp.int32, (c, LANES), 0) // lseq
    zero64 = jnp.zeros((HEAD_DIM, HEAD_DIM), F32)

    if has_s0:
        for pp in range(npp):
            for s in range(nsq):
                top = jnp.concatenate([s0_ref[s, 2 * pp], zero64], axis=1)
                bot = jnp.concatenate([zero64, s0_ref[s, 2 * pp + 1]], axis=1)
                st_ref[pp, :, s * LANES:(s + 1) * LANES] = jnp.concatenate([top, bot], axis=0).T
    else:
        st_ref[...] = jnp.zeros_like(st_ref)

    def expand(x):
        if nsq == 1:
            return x
        return jnp.concatenate([jnp.where(row_seq == s, x, 0.0) for s in range(nsq)], axis=1)

    def heads_on_rows(x):
        return jnp.concatenate([jnp.where(lane_lo, x, 0.0), jnp.where(lane_lo, 0.0, x)], axis=0)

    def chunk(ci, carry):
        r0 = pl.multiple_of(ci * c, c)
        for pp in range(npp):
            ls = slice(pp * LANES, (pp + 1) * LANES)
            lbv = lb_ref[:, ls]
            qa = q_ref[pl.ds(r0, c), ls]
            z = f_ref[pl.ds(r0, c), ls]
            v = i_ref[pl.ds(r0, c), ls]
            lf = jnp.log(lbv + (1.0 - lbv) * jax.nn.sigmoid(z))
            kin = (1.0 - lbv) * jax.nn.sigmoid(-z)
            qs = qa * jax.nn.sigmoid(qa)
            hi = lf.astype(BF16)
            lo = (lf - hi.astype(F32)).astype(BF16)
            g = _dot(a_ref[...], jnp.concatenate([hi, lo], axis=0))
            eb = jnp.exp(g[0:c])
            qg = qs * eb
            kd = kin * jnp.exp(g[c:2 * c])
            st = st_ref[pp]
            o = _dot_nt(expand(qg).astype(BF16), st.astype(BF16))

            kin_b = kin.astype(BF16)
            sc = mb_ref[nlev] * _dot_nt(heads_on_rows(qs).astype(BF16), kin_b)
            for lev in range(nlev):
                y = jnp.exp(g[(2 + lev) * c:(3 + lev) * c])
                isq = rq_ref[lev]
                ql = qs * (y * isq)
                kl = kin * (y * (1.0 - isq))
                sc = sc + mb_ref[lev] * _dot_nt(heads_on_rows(ql).astype(BF16), kl.astype(BF16))
            r = _dot(sc.astype(BF16), v.astype(BF16))
            o_ref[pl.ds(r0, c), ls] = o + jnp.where(lane_lo, r[0:c], r[c:2 * c])

            if nsq == 1:
                dl = eb[c - 1:c, :]
            else:
                dl = jnp.concatenate([eb[(s + 1) * lseq - 1:(s + 1) * lseq, :] for s in range(nsq)], axis=1)
            u = _dot(v.T.astype(BF16), expand(kd).astype(BF16))
            st_ref[pp] = st * dl + jnp.where(same_head, u, 0.0)
        return carry

    lax.fori_loop(0, nch, chunk, 0)

    for pp in range(npp):
        for s in range(nsq):
            sbd = st_ref[pp, :, s * LANES:(s + 1) * LANES].T
            s_ref[s, 2 * pp] = sbd[0:HEAD_DIM, 0:HEAD_DIM]
            s_ref[s, 2 * pp + 1] = sbd[HEAD_DIM:, HEAD_DIM:]


def _hgrn(proj, lb, s0, nseq, t):
    _, n, width = proj.shape
    npair = width // LANES
    c = LANES
    lseq = min(t, c)
    nsq = c // lseq
    if t >= c:
        nch, rows, ngroups = t // c, t, nseq
    else:
        nch, rows, ngroups = 1, c, nseq // nsq
    npp = 4 if (nsq == 1 and npair % 4 == 0) else 1
    wl = npp * LANES
    a, rq, mb, nlev = _hgrn_consts(c, lseq)
    has_s0 = s0 is not None
    ins = [proj, proj, proj, lb.reshape(1, width), a, rq, mb]
    in_specs = [pl.BlockSpec((None, rows, wl), lambda g, p: (0, g, p)),
                pl.BlockSpec((None, rows, wl), lambda g, p: (1, g, p)),
                pl.BlockSpec((None, rows, wl), lambda g, p: (2, g, p)),
                pl.BlockSpec((1, wl), lambda g, p: (0, p)),
                pl.BlockSpec(a.shape, lambda g, p: (0, 0)),
                pl.BlockSpec(rq.shape, lambda g, p: (0, 0, 0)),
                pl.BlockSpec(mb.shape, lambda g, p: (0, 0, 0))]
    if has_s0:
        ins.append(s0)
        in_specs.append(pl.BlockSpec((nsq, 2 * npp, HEAD_DIM, HEAD_DIM), lambda g, p: (g, p, 0, 0)))
    kern = functools.partial(_hgrn_kernel, c=c, lseq=lseq, nch=nch, nlev=nlev, has_s0=has_s0, npp=npp)
    return pl.pallas_call(
        kern,
        out_shape=(jax.ShapeDtypeStruct((n, width), F32),
                   jax.ShapeDtypeStruct((nseq, 2 * npair, HEAD_DIM, HEAD_DIM), F32)),
        grid=(ngroups, npair // npp),
        in_specs=in_specs,
        out_specs=(pl.BlockSpec((rows, wl), lambda g, p: (g, p)),
                   pl.BlockSpec((nsq, 2 * npp, HEAD_DIM, HEAD_DIM), lambda g, p: (g, p, 0, 0))),
        scratch_shapes=[pltpu.VMEM((npp, LANES, nsq * LANES), F32)],
        compiler_params=_params("parallel", "parallel"),
        name="hgrn",
    )(*ins)


LOG2E = 1.4426950408889634


def _sb_block(z2, tail, uneg, mask):
    sp = jnp.maximum(z2, 0.0) + jnp.log2(1.0 + jnp.exp2(-jnp.abs(z2)))
    if mask is not None:
        sp = jnp.where(mask, sp, 0.0)
    rr = _dot(sp.astype(BF16), uneg)
    a = jnp.exp2(z2 - sp + (tail + rr))
    if mask is not None:
        a = jnp.where(mask, a, 0.0)
    return a, tail - jnp.sum(sp, axis=1, keepdims=True)


def _uneg_const(tk):
    j = np.arange(tk)
    return jnp.asarray(-(j[:, None] > j[None, :]).astype(np.float32), BF16)


def _sb_kernel(q_ref, k_ref, v_ref, bias_ref, u2_ref, o_ref, *, tq, tk, npp):
    qi = pl.program_id(2)
    u2 = u2_ref[...]
    scale = HEAD_DIM ** -0.5 * LOG2E
    lane_lo = lax.broadcasted_iota(jnp.int32, (tq, LANES), 1) < HEAD_DIM
    ndiag = tq // tk
    base = qi * ndiag
    below_diag = (lax.broadcasted_iota(jnp.int32, (tk, tk), 1) < lax.broadcasted_iota(jnp.int32, (tk, tk), 0))
    lanes = [slice(pp * LANES, (pp + 1) * LANES) for pp in range(npp)]
    qh, bias = [], []
    for pp in range(npp):
        q = q_ref[:, lanes[pp]] * scale
        qh.append((jnp.where(lane_lo, q, 0.0).astype(BF16), jnp.where(lane_lo, 0.0, q).astype(BF16)))
        h0 = 2 * (pl.program_id(1) * npp + pp)
        bias.append((bias_ref[h0] * LOG2E, bias_ref[h0 + 1] * LOG2E))

    def logits(qpair, kb, pp):
        kk = k_ref[pl.ds(pl.multiple_of(kb * tk, tk), tk), lanes[pp]].astype(BF16)
        return tuple(_dot_nt(qpair[h], kk) + bias[pp][h] for h in range(2))

    def finish(z, kb, carry, mask, pp):
        vv = v_ref[pl.ds(pl.multiple_of(kb * tk, tk), tk), lanes[pp]].astype(BF16)
        out = []
        for h in range(2):
            acc, tail = carry[h]
            a, tail = _sb_block(z[h], tail, u2, mask)
            out.append((acc + _dot(a.astype(BF16), vv), tail))
        return tuple(out)

    carries = []
    for pp in range(npp):
        bands = []
        for r in range(ndiag):
            qband = tuple(x[r * tk:(r + 1) * tk] for x in qh[pp])
            zero = (jnp.zeros((tk, LANES), F32), jnp.zeros((tk, 1), F32))
            c = finish(logits(qband, base + r, pp), base + r, (zero, zero), below_diag, pp)
            for kb in range(r - 1, -1, -1):
                c = finish(logits(qband, base + kb, pp), base + kb, c, None, pp)
            bands.append(c)
        carries.append(tuple(tuple(jnp.concatenate([b[h][j] for b in bands], axis=0) for j in range(2))
                             for h in range(2)))

    def body(i, state):
        cs, zs = state
        kb = base - 1 - i
        z_next = tuple(logits(qh[pp], jnp.maximum(kb - 1, 0), pp) for pp in range(npp))
        return tuple(finish(zs[pp], kb, cs[pp], None, pp) for pp in range(npp)), z_next

    z0 = tuple(logits(qh[pp], jnp.maximum(base - 1, 0), pp) for pp in range(npp))
    carries, _ = lax.fori_loop(0, base, body, (tuple(carries), z0))
    for pp in range(npp):
        o_ref[:, lanes[pp]] = jnp.where(lane_lo, carries[pp][0][0], carries[pp][1][0])


def _sb_prompt(proj, bias, nseq, t, tq, tk):
    _, n, width = proj.shape
    npair = width // LANES
    npp = 2 if npair % 2 == 0 else 1
    wl = npp * LANES
    nq = t // tq
    u2 = _uneg_const(tk)
    kern = functools.partial(_sb_kernel, tq=tq, tk=tk, npp=npp)
    return pl.pallas_call(
        kern,
        out_shape=jax.ShapeDtypeStruct((n, width), F32),
        grid=(nseq, npair // npp, nq),
        in_specs=[pl.BlockSpec((None, tq, wl), lambda b, p, i: (4, b * nq + i, p)),
                  pl.BlockSpec((None, t, wl), lambda b, p, i: (5, b, p)),
                  pl.BlockSpec((None, t, wl), lambda b, p, i: (6, b, p)),
                  pl.BlockSpec(memory_space=pltpu.SMEM),
                  pl.BlockSpec(u2.shape, lambda b, p, i: (0, 0))],
        out_specs=pl.BlockSpec((tq, wl), lambda b, p, i: (b * nq + i, p)),
        compiler_params=_params("parallel", "parallel", "arbitrary"),
        name="sb",
    )(proj, proj, proj, bias, u2)


def _sbpaged_kernel(pt_ref, q_ref, kn_ref, vn_ref, bias_ref, u2_ref, *rest, npages, page, t, nh):
    k_refs = rest[:npages]
    v_refs = rest[npages:2 * npages]
    o_ref = rest[2 * npages]
    rows = nh * t
    scale = HEAD_DIM ** -0.5 * LOG2E
    u2 = u2_ref[...]
    qa = (q_ref[...] * scale).astype(BF16)
    bias = bias_ref[...]
    head_z = lax.broadcasted_iota(jnp.int32, (rows, page), 0) // t
    head_o = lax.broadcasted_iota(jnp.int32, (rows, HEAD_DIM), 0) // t

    order = list(range(npages - 1, -1, -1))
    kblk = [kn_ref] + [k_refs[p] for p in order]
    vblk = [vn_ref] + [v_refs[p] for p in order]
    nblk = npages + 1

    def logits(x_ref):
        z = bias
        for h in range(nh):
            z = z + jnp.where(head_z == h, _dot(qa, x_ref[h].astype(BF16)), 0.0)
        return z

    z2 = jnp.concatenate([logits(x) for x in kblk], axis=0)
    qpos = lax.broadcasted_iota(jnp.int32, (rows, page), 0) % t
    kpos = lax.broadcasted_iota(jnp.int32, (rows, page), 1)
    visible = jnp.concatenate([kpos < qpos] + [jnp.ones((rows, page), jnp.bool_)] * npages, axis=0)
    sp = jnp.where(visible, jnp.maximum(z2, 0.0) + jnp.log2(1.0 + jnp.exp2(-jnp.abs(z2))), 0.0)
    rr = _dot(sp.astype(BF16), u2)
    blk_sum = jnp.sum(sp, axis=1, keepdims=True)
    tails = [jnp.zeros((rows, 1), F32)]
    for b in range(nblk - 1):
        tails.append(tails[-1] - blk_sum[b * rows:(b + 1) * rows])
    a = jnp.where(visible, jnp.exp2(z2 - sp + (jnp.concatenate(tails, axis=0) + rr)), 0.0).astype(BF16)
    acc = jnp.zeros((rows, HEAD_DIM), F32)
    for b in range(nblk):
        a_b = a[b * rows:(b + 1) * rows]
        for h in range(nh):
            acc = acc + jnp.where(head_o == h, _dot_nt(a_b, vblk[b][h].astype(BF16)), 0.0)
    o_ref[...] = acc


def _sb_paged(q, kn_t, vn_t, cache_kt, cache_vt, layer, page_table, sb_bias):
    nseq, rows, _ = q.shape
    nh, page = cache_kt.shape[2], cache_kt.shape[4]
    t = rows // nh
    npages = page_table.shape[1]
    u2 = _uneg_const(page)
    bias = jnp.broadcast_to(jnp.repeat(sb_bias.astype(F32) * LOG2E, t)[:, None], (rows, page))
    kern = functools.partial(_sbpaged_kernel, npages=npages, page=page, t=t, nh=nh)

    def page_spec(p):
        return pl.BlockSpec((None, None, nh, HEAD_DIM, page), lambda b, pt: (layer, pt[b, p], 0, 0, 0))

    new_spec = pl.BlockSpec((None, nh, HEAD_DIM, page), lambda b, pt: (b, 0, 0, 0))
    grid_spec = pltpu.PrefetchScalarGridSpec(
        num_scalar_prefetch=1,
        grid=(nseq,),
        in_specs=[pl.BlockSpec((None, rows, HEAD_DIM), lambda b, pt: (b, 0, 0)),
                  new_spec, new_spec,
                  pl.BlockSpec((rows, page), lambda b, pt: (0, 0)),
                  pl.BlockSpec(u2.shape, lambda b, pt: (0, 0))]
                 + [page_spec(p) for p in range(npages)] * 2,
        out_specs=pl.BlockSpec((None, rows, HEAD_DIM), lambda b, pt: (b, 0, 0)),
    )
    return pl.pallas_call(
        kern,
        out_shape=jax.ShapeDtypeStruct((nseq, rows, HEAD_DIM), F32),
        grid_spec=grid_spec,
        compiler_params=_params("arbitrary"),
        name="sbpaged",
    )(page_table, q, kn_t, vn_t, bias, u2, *([cache_kt] * npages), *([cache_vt] * npages))


def _layernorm_rows(x, g, b):
    mu = jnp.mean(x, axis=-1, keepdims=True)
    xc = x - mu
    var = jnp.mean(xc * xc, axis=-1, keepdims=True)
    return xc * lax.rsqrt(var + LN_EPS) * g + b


def _head_rms(o, hsum):
    sq = o * o
    hi = sq.astype(BF16)
    lo = (sq - hi.astype(F32)).astype(BF16)
    ms = (_dot(hi, hsum) + _dot(lo, hsum)) * (1.0 / HEAD_DIM)
    return o * lax.rsqrt(ms + RMS_EPS)


def _outproj_kernel(oa_ref, ga_ref, ob_ref, x_ref, g1_ref, sh2_ref, sc2_ref, na_ref, nb_ref, hs_ref,
                    w_ref, lg_ref, lb_ref, x1_ref, h2t_ref, *, alpha):
    hs = hs_ref[...]
    ga = ga_ref[...]
    ya = _head_rms(oa_ref[...], hs) * na_ref[...] * (ga * jax.nn.sigmoid(ga))
    yb = _head_rms(ob_ref[...], hs) * nb_ref[...]
    wa = w_ref.shape[0] // 2
    mix = _dot(ya.astype(BF16), w_ref[0:wa, :]) + _dot(yb.astype(BF16), w_ref[wa:, :])
    x1 = _layernorm_rows(alpha * x_ref[...] + (1.0 + g1_ref[...]) * mix, lg_ref[...], lb_ref[...])
    x1_ref[...] = x1
    h2 = x1 * (1.0 + sc2_ref[...]) + sh2_ref[...]
    h2t_ref[...] = h2.T.astype(BF16)


def _outproj(oa, proj, ob, x, ms, norm_a, norm_b, w_out_bf16, ln_g, ln_b, alpha):
    n, d = x.shape
    tm = ms.tm
    wa = oa.shape[1]
    hs = (np.arange(wa)[:, None] // HEAD_DIM == np.arange(wa)[None, :] // HEAD_DIM).astype(np.float32)
    row = lambda i: (i, 0)
    fixed = lambda i: (0, 0)
    return pl.pallas_call(
        functools.partial(_outproj_kernel, alpha=alpha),
        out_shape=(jax.ShapeDtypeStruct((n, d), F32), jax.ShapeDtypeStruct((d, n), BF16)),
        grid=(n // tm,),
        in_specs=[pl.BlockSpec((tm, wa), row),
                  pl.BlockSpec((None, tm, wa), lambda i: (3, i, 0)),
                  pl.BlockSpec((tm, wa), row),
                  pl.BlockSpec((tm, d), row),
                  ms.spec(2, 1), ms.spec(3, 1), ms.spec(4, 1),
                  pl.BlockSpec((1, wa), fixed), pl.BlockSpec((1, wa), fixed),
                  pl.BlockSpec((wa, wa), fixed),
                  pl.BlockSpec((2 * wa, d), fixed),
                  pl.BlockSpec((1, d), fixed), pl.BlockSpec((1, d), fixed)],
        out_specs=(pl.BlockSpec((tm, d), row), pl.BlockSpec((d, tm), lambda i: (0, i))),
        compiler_params=_params("parallel"),
        name="outproj",
    )(oa, proj, ob, x, ms.arr, ms.arr, ms.arr, norm_a.reshape(1, wa), norm_b.reshape(1, wa),
      jnp.asarray(hs, BF16), w_out_bf16, ln_g.reshape(1, d), ln_b.reshape(1, d))


MARK0 = -(2.0 ** 127)
MSTEP = 2.0 ** 105
CAND_ROWS = 64


def _stair_cells():
    k = PEER_TOPK
    return [(a, b) for a in range(k) for b in range(k) if (a + 1) * (b + 1) <= k]


def _cand_consts():
    cells = _stair_cells()
    g1 = np.zeros((CAND_ROWS, PEER_TOPK), np.float32)
    g2 = np.zeros((CAND_ROWS, PEER_TOPK), np.float32)
    for r, (a, b) in enumerate(cells):
        g1[r, a] = 1.0
        g2[r, b] = 1.0
    return jnp.asarray(g1, BF16), jnp.asarray(g2, BF16), jnp.asarray(g1.T.copy(), BF16), len(cells)


def _extract_topk(s, k):
    rows, tn = s.shape
    ridx = lax.broadcasted_iota(jnp.int32, (rows, tn), 0)
    kidx = lax.broadcasted_iota(jnp.int32, (k, tn), 0)

    def body(i, carry):
        cur, vals, rank = carry
        m = jnp.max(cur, axis=0, keepdims=True)
        first = jnp.min(jnp.where(cur == m, ridx, rows), axis=0, keepdims=True)
        hit = ridx == first
        cur = jnp.where(hit, -jnp.inf, cur)
        rank = jnp.where(hit, jnp.full((1, tn), i, jnp.int32).astype(F32), rank)
        vals = jnp.where(kidx == i, m, vals)
        return cur, vals, rank

    init = (s, jnp.zeros((k, tn), F32), jnp.full((rows, tn), float(k), F32))
    _, vals, rank = lax.fori_loop(0, k, body, init)
    return vals, rank


def _knock_topk(cur_ref, val_ref, narr, k):
    tn = cur_ref.shape[2]

    def body(i, carry):
        mark = MARK0 - jnp.full((1, tn), i, jnp.int32).astype(F32) * MSTEP
        for a in range(narr):
            cur = cur_ref[a]
            m = jnp.max(cur, axis=0, keepdims=True)
            cur_ref[a] = jnp.where(cur == m, mark, cur)
            val_ref[a, pl.ds(i, 1), :] = m
        return carry

    lax.fori_loop(0, k, body, 0)


def _tie_free(cur_ref, narr, k, valid=None):
    ok = None
    for a in range(narr):
        gone = cur_ref[a] <= MARK0
        if valid is not None:
            gone = gone & valid
        good = jnp.sum(gone.astype(F32), axis=0, keepdims=True) == float(k)
        ok = good if ok is None else (ok & good)
    return jnp.min(ok.astype(jnp.int32)) == 1


def _exact_topk_into(src_ref, cur_ref, val_ref, narr, k):
    for a in range(narr):
        s = src_ref[a]
        vals, rank = _extract_topk(s, k)
        val_ref[a] = vals
        cur_ref[a] = jnp.where(rank < float(k), MARK0 - rank * MSTEP, s)


def _bcast_rows_bf16(row, nrows):
    tile = jnp.broadcast_to(row, (16, row.shape[1])).astype(BF16)
    return jnp.concatenate([tile] * (nrows // 16), axis=0)


def _peertopk_kernel(ht_ref, wqt_ref, k1_ref, k2_ref, g1_ref, g2_ref, ga_ref, c1_ref, a1_ref, r2_ref, a2_ref,
                     s_scr, cur_scr, val_scr, cand_scr, ccur_scr, cval_scr, *, nheads, nkeys, ncells):
    tn = ht_ref.shape[1]
    dq = k1_ref.shape[1]
    k = PEER_TOPK
    qt = _dot(wqt_ref[...], ht_ref[...]).astype(BF16)
    k1 = k1_ref[...].astype(BF16)
    k2 = k2_ref[...].astype(BF16)
    for h in range(nheads):
        s_scr[2 * h] = _dot(k1, qt[(2 * h) * dq:(2 * h + 1) * dq, :])
        s_scr[2 * h + 1] = _dot(k2, qt[(2 * h + 1) * dq:(2 * h + 2) * dq, :])
    cur_scr[...] = s_scr[...]
    _knock_topk(cur_scr, val_scr, 2 * nheads, k)

    @pl.when(jnp.logical_not(_tie_free(cur_scr, 2 * nheads, k)))
    def _():
        _exact_topk_into(s_scr, cur_scr, val_scr, 2 * nheads, k)

    g1, g2 = g1_ref[...], g2_ref[...]
    valid = lax.broadcasted_iota(jnp.int32, (CAND_ROWS, tn), 0) < ncells

    def pick(g, v):
        hi, mid, lo = _split3(v)
        return _dot(g, hi) + _dot(g, mid) + _dot(g, lo)

    for h in range(nheads):
        cand = jnp.where(valid, pick(g1, val_scr[2 * h]) + pick(g2, val_scr[2 * h + 1]), -jnp.inf)
        cand_scr[h] = cand
        ccur_scr[h] = cand
    _knock_topk(ccur_scr, cval_scr, nheads, k)

    @pl.when(jnp.logical_not(_tie_free(ccur_scr, nheads, k, valid)))
    def _():
        _exact_topk_into(cand_scr, ccur_scr, cval_scr, nheads, k)

    ga = ga_ref[...]
    inv_step = 1.0 / MSTEP
    for h in range(nheads):
        taken = ((ccur_scr[h] <= MARK0) & valid).astype(BF16)
        cnt = _dot(ga, taken)
        top = cval_scr[h]
        inv_z = 1.0 / jnp.sum(jnp.exp(top - top[0:1]), axis=0, keepdims=True)
        s1, s2 = s_scr[2 * h], s_scr[2 * h + 1]
        cur1, cur2 = cur_scr[2 * h], cur_scr[2 * h + 1]
        sel1, sel2 = cur1 <= MARK0, cur2 <= MARK0
        rank1 = jnp.where(sel1, (MARK0 - cur1) * inv_step, float(k)).astype(BF16)
        rank2 = jnp.where(sel2, (MARK0 - cur2) * inv_step, float(k))
        c1 = jnp.zeros((nkeys, tn), BF16)
        for a in range(k):
            c1 = jnp.where(rank1 == a, _bcast_rows_bf16(cnt[a:a + 1], nkeys), c1)
        v1 = val_scr[2 * h]
        v2 = val_scr[2 * h + 1]
        c1_ref[h] = c1.astype(F32)
        a1_ref[h] = jnp.where(sel1, jnp.exp(s1 - v1[0:1]) * inv_z, 0.0)
        r2_ref[h] = rank2.astype(BF16)
        a2_ref[h] = jnp.where(sel2, jnp.exp(s2 - v2[0:1]), 0.0).astype(BF16)


def _peer_topk(h2t, wqt_bf16, k1, k2, tn):
    d, n = h2t.shape
    nkeys, dq = k1.shape
    nheads = wqt_bf16.shape[0] // (2 * dq)
    g1, g2, ga, ncells = _cand_consts()
    shp = (nheads, nkeys, n)
    blk = pl.BlockSpec((nheads, nkeys, tn), lambda i: (0, 0, i))
    fixed = lambda i: (0, 0)
    k = PEER_TOPK
    return pl.pallas_call(
        functools.partial(_peertopk_kernel, nheads=nheads, nkeys=nkeys, ncells=ncells),
        out_shape=(jax.ShapeDtypeStruct(shp, F32), jax.ShapeDtypeStruct(shp, F32),
                   jax.ShapeDtypeStruct(shp, BF16), jax.ShapeDtypeStruct(shp, BF16)),
        grid=(n // tn,),
        in_specs=[pl.BlockSpec((d, tn), lambda i: (0, i)),
                  pl.BlockSpec(wqt_bf16.shape, fixed),
                  pl.BlockSpec((nkeys, dq), fixed), pl.BlockSpec((nkeys, dq), fixed),
                  pl.BlockSpec(g1.shape, fixed), pl.BlockSpec(g2.shape, fixed), pl.BlockSpec(ga.shape, fixed)],
        out_specs=(blk, blk, blk, blk),
        scratch_shapes=[pltpu.VMEM((2 * nheads, nkeys, tn), F32), pltpu.VMEM((2 * nheads, nkeys, tn), F32),
                        pltpu.VMEM((2 * nheads, k, tn), F32),
                        pltpu.VMEM((nheads, CAND_ROWS, tn), F32), pltpu.VMEM((nheads, CAND_ROWS, tn), F32),
                        pltpu.VMEM((nheads, k, tn), F32)],
        compiler_params=_params("parallel"),
        name="peertopk",
    )(h2t, wqt_bf16, k1, k2, g1, g2, ga)


GATE_LANES = 256


def _gelu_tanh(x):
    c = 0.7978845608028654
    hx = 0.5 * x
    return hx + hx * jnp.tanh(x * (c + (c * 0.044715) * (x * x)))


def _peermain_kernel(ht_ref, u_ref, vt_ref, c1_ref, a1_ref, r2_ref, a2_ref, o_ref, act_ref, g_ref, *, nkeys, nheads):
    j = pl.program_id(1)
    te, tn = act_ref.shape
    gl = min(GATE_LANES, tn)
    assert tn % gl == 0
    zero = jnp.zeros((), BF16)

    @pl.when(j == 0)
    def _():
        o_ref[...] = jnp.zeros_like(o_ref)

    act_ref[...] = _dot(u_ref[...], ht_ref[...])
    for sub in range(te // nkeys):
        arow = slice(sub * nkeys, (sub + 1) * nkeys)
        for lt in range(tn // gl):
            cols = slice(lt * gl, (lt + 1) * gl)
            w = jnp.zeros((nkeys, gl), BF16)
            for h in range(nheads):
                c1 = _bcast_rows_bf16(c1_ref[h, sub:sub + 1, cols], nkeys)
                a1 = _bcast_rows_bf16(a1_ref[h, sub:sub + 1, cols], nkeys)
                w = w + a1 * jnp.where(r2_ref[h, :, cols] < c1, a2_ref[h, :, cols], zero)
            g_ref[arow, cols] = w * _gelu_tanh(act_ref[arow, cols]).astype(BF16)
    o_ref[...] += _dot(vt_ref[...], g_ref[...])


def _peer_main(h2t, u_bf16, vt_bf16, c1, a1, r2, a2, tn, te):
    d, n = h2t.shape
    ne = u_bf16.shape[0]
    nheads, nkeys, _ = c1.shape
    small = pl.BlockSpec((nheads, nkeys, tn), lambda i, j: (0, 0, i))
    per_tile = pl.BlockSpec((nheads, te // nkeys, tn), lambda i, j: (0, j, i))
    return pl.pallas_call(
        functools.partial(_peermain_kernel, nkeys=nkeys, nheads=nheads),
        out_shape=jax.ShapeDtypeStruct((d, n), F32),
        grid=(n // tn, ne // te),
        in_specs=[pl.BlockSpec((d, tn), lambda i, j: (0, i)),
                  pl.BlockSpec((te, d), lambda i, j: (j, 0)),
                  pl.BlockSpec((d, te), lambda i, j: (0, j)),
                  per_tile, per_tile, small, small],
        out_specs=pl.BlockSpec((d, tn), lambda i, j: (0, i)),
        scratch_shapes=[pltpu.VMEM((te, tn), F32), pltpu.VMEM((te, tn), BF16)],
        compiler_params=_params("parallel", "arbitrary"),
        name="peermain",
    )(h2t, u_bf16, vt_bf16, c1, a1, r2, a2)


def _ln2_kernel(x1_ref, fft_ref, g2_ref, lg_ref, lb_ref, y_ref, *, alpha):
    ff = fft_ref[...].T
    y_ref[...] = _layernorm_rows(alpha * x1_ref[...] + (1.0 + g2_ref[...]) * ff, lg_ref[...], lb_ref[...])


def _ln2(x1, fft, col0, ms, ln_g, ln_b, alpha):
    n, d = x1.shape
    tm = ms.tm
    c0 = col0 // tm
    return pl.pallas_call(
        functools.partial(_ln2_kernel, alpha=alpha),
        out_shape=jax.ShapeDtypeStruct((n, d), F32),
        grid=(n // tm,),
        in_specs=[pl.BlockSpec((tm, d), lambda i: (i, 0)),
                  pl.BlockSpec((d, tm), lambda i: (0, c0 + i)),
                  ms.spec(5, 1),
                  pl.BlockSpec((1, d), lambda i: (0, 0)), pl.BlockSpec((1, d), lambda i: (0, 0))],
        out_specs=pl.BlockSpec((tm, d), lambda i: (i, 0)),
        compiler_params=_params("parallel"),
        name="ln2",
    )(x1, fft, ms.arr, ln_g.reshape(1, d), ln_b.reshape(1, d))


def _pick(n, pref):
    for t in pref:
        if n % t == 0:
            return t
    raise ValueError(f"no tile for {n}")


def kernel(x_prompt, x_sample, cache_k, cache_v, state_hgrn, page_table, c_prompt, c_sample, w_ada, b_ada, w_in, lb_raw, norm_a, norm_b, sb_bias, w_out, ln1_g, ln1_b, ln2_g, ln2_b, peer_wq, peer_k1, peer_k2, peer_u, peer_v):
    depth = w_ada.shape[0]
    bp, tp, d = x_prompt.shape
    bs, ts, _ = x_sample.shape
    n_p, n_s = bp * tp, bs * ts
    page, hb = cache_k.shape[2], cache_k.shape[3]
    wb = hb * HEAD_DIM
    alpha = (2.0 * depth) ** 0.25
    lb_all = jnp.cumsum(jax.nn.softmax(lb_raw.astype(F32), axis=0), axis=0)

    tm_p = _pick(tp, (1024, 512, 256, 128))
    tm_s = _pick(n_s, (1024, 512, 256, 128))
    assert n_p % tm_s == 0
    n_all = n_p + n_s
    tn_k = _pick(n_all, (256, 128))
    tn_m = _pick(n_all, (512, 256, 128))
    tq = _pick(tp, (512, 256, 128))
    tk = min(tq, 256)

    cache_kt = jnp.transpose(cache_k, (0, 1, 3, 4, 2))
    cache_vt = jnp.transpose(cache_v, (0, 1, 3, 4, 2))
    yp = x_prompt.reshape(n_p, d)
    ys = x_sample.reshape(n_s, d)
    outs = {k: [] for k in ("kp", "vp", "sp", "ks", "vs", "ss")}
    for l in range(depth):
        mod = _mod(jnp.concatenate([c_prompt, c_sample], axis=0), w_ada[l], b_ada[l])
        ms_p = _ModSpec(mod[:bp].reshape(bp, 1, 6 * d), False, tm_p, tp // tm_p, d)
        ms_s = _ModSpec(jnp.repeat(mod[bp:], ts, axis=0), True, tm_s, 1, d)
        w_in_b = w_in[l].astype(BF16)
        w_out_b = w_out[l].astype(BF16)
        lb = lb_all[l]

        proj_p = _inproj(yp, ms_p, w_in_b, wb)
        proj_s = _inproj(ys, ms_s, w_in_b, wb)

        oa_p, sp = _hgrn(proj_p, lb, None, bp, tp)
        oa_s, ss = _hgrn(proj_s, lb, state_hgrn[l], bs, ts)

        ob_p = _sb_prompt(proj_p, sb_bias[l].astype(F32), bp, tp, tq, tk)
        q_hi = proj_s[4].reshape(bs, ts, hb, HEAD_DIM).transpose(0, 2, 1, 3).reshape(bs, hb * ts, HEAD_DIM)
        new_t = lambda a: jnp.pad(a.reshape(bs, ts, hb, HEAD_DIM).transpose(0, 2, 3, 1),
                                  ((0, 0), (0, 0), (0, 0), (0, page - ts)))
        ob_s = _sb_paged(q_hi, new_t(proj_s[5]), new_t(proj_s[6]), cache_kt, cache_vt, l, page_table, sb_bias[l])
        ob_s = ob_s.reshape(bs, hb, ts, HEAD_DIM).transpose(0, 2, 1, 3).reshape(n_s, wb)

        x1_p, h2t_p = _outproj(oa_p, proj_p, ob_p, yp, ms_p, norm_a[l], norm_b[l], w_out_b, ln1_g[l], ln1_b[l], alpha)
        x1_s, h2t_s = _outproj(oa_s, proj_s, ob_s, ys, ms_s, norm_a[l], norm_b[l], w_out_b, ln1_g[l], ln1_b[l], alpha)

        h2t = jnp.concatenate([h2t_p, h2t_s], axis=1)
        c1, a1, r2, a2 = _peer_topk(h2t, peer_wq[l].T.astype(BF16), peer_k1[l], peer_k2[l], tn_k)
        fft = _peer_main(h2t, peer_u[l].astype(BF16), peer_v[l].T.astype(BF16), c1, a1, r2, a2,
                         tn_m, 16 * peer_k1.shape[1])

        yp = _ln2(x1_p, fft, 0, ms_p, ln2_g[l], ln2_b[l], alpha)
        ys = _ln2(x1_s, fft, n_p, ms_s, ln2_g[l], ln2_b[l], alpha)

        outs["kp"].append(proj_p[5].reshape(bp, tp, hb, HEAD_DIM))
        outs["vp"].append(proj_p[6].reshape(bp, tp, hb, HEAD_DIM))
        outs["sp"].append(sp)
        outs["ks"].append(proj_s[5].reshape(bs, ts, hb, HEAD_DIM))
        outs["vs"].append(proj_s[6].reshape(bs, ts, hb, HEAD_DIM))
        outs["ss"].append(ss)
    return (yp.reshape(bp, tp, d), ys.reshape(bs, ts, d),
            jnp.stack(outs["kp"]), jnp.stack(outs["vp"]), jnp.stack(outs["sp"]),
            jnp.stack(outs["ks"]), jnp.stack(outs["vs"]), jnp.stack(outs["ss"]))
```

```python
import functools
import math

import numpy as np
import jax
import jax.numpy as jnp
from jax import lax
from jax.experimental import pallas as pl
from jax.experimental.pallas import tpu as pltpu

F32 = jnp.float32
BF16 = jnp.bfloat16

HEAD_DIM = 64
LANES = 128
LN_EPS = 1e-5
RMS_EPS = 1e-6
PEER_TOPK = 16
VMEM_LIMIT = 56 * 1024 * 1024

_NT = (((1,), (1,)), ((), ()))
_TN = (((0,), (0,)), ((), ()))


def _dot(a, b):
    return jnp.dot(a, b, preferred_element_type=F32)


def _dot_nt(a, b):
    return lax.dot_general(a, b, _NT, preferred_element_type=F32)


def _split3(x):
    hi = x.astype(BF16)
    r1 = x - hi.astype(F32)
    mid = r1.astype(BF16)
    lo = (r1 - mid.astype(F32)).astype(BF16)
    return hi, mid, lo


def _params(*sem):
    return pltpu.CompilerParams(dimension_semantics=sem, vmem_limit_bytes=VMEM_LIMIT)


def _mod_kernel(c_ref, w_ref, b_ref, o_ref):
    o_ref[...] = _dot(c_ref[...].astype(BF16), w_ref[...].astype(BF16)) + b_ref[...]


def _mod(c, w, b):
    n, d = c.shape
    width = w.shape[1]
    tn = 1024
    return pl.pallas_call(
        _mod_kernel,
        out_shape=jax.ShapeDtypeStruct((n, width), F32),
        grid=(width // tn,),
        in_specs=[pl.BlockSpec((n, d), lambda j: (0, 0)),
                  pl.BlockSpec((d, tn), lambda j: (0, j)),
                  pl.BlockSpec((1, tn), lambda j: (0, j))],
        out_specs=pl.BlockSpec((n, tn), lambda j: (0, j)),
        compiler_params=_params("parallel"),
        name="mod",
    )(c, w, b.reshape(1, width))


class _ModSpec:
    def __init__(self, arr, per_token, tm, tiles_per_seq, d):
        self.arr, self.per_token, self.tm, self.tps, self.d = arr, per_token, tm, tiles_per_seq, d

    def spec(self, part, ngrid):
        d, tps = self.d, self.tps
        if self.per_token:
            if ngrid == 1:
                return pl.BlockSpec((self.tm, d), lambda i: (i, part))
            return pl.BlockSpec((self.tm, d), lambda i, j: (i, part))
        if ngrid == 1:
            return pl.BlockSpec((None, 1, d), lambda i: (i // tps, 0, part))
        return pl.BlockSpec((None, 1, d), lambda i, j: (i // tps, 0, part))


def _inproj_kernel(x_ref, sh_ref, sc_ref, w_ref, o_ref, h_ref):
    @pl.when(pl.program_id(1) == 0)
    def _():
        h_ref[...] = (x_ref[...] * (1.0 + sc_ref[...]) + sh_ref[...]).astype(BF16)

    o_ref[...] = _dot(h_ref[...], w_ref[...])


def _inproj(x, ms, w_bf16, tn):
    n, d = x.shape
    tm = ms.tm
    nparts = w_bf16.shape[1] // tn
    return pl.pallas_call(
        _inproj_kernel,
        out_shape=jax.ShapeDtypeStruct((nparts, n, tn), F32),
        grid=(n // tm, nparts),
        in_specs=[pl.BlockSpec((tm, d), lambda i, j: (i, 0)),
                  ms.spec(0, 2), ms.spec(1, 2),
                  pl.BlockSpec((d, tn), lambda i, j: (0, j))],
        out_specs=pl.BlockSpec((None, tm, tn), lambda i, j: (j, i, 0)),
        scratch_shapes=[pltpu.VMEM((tm, d), BF16)],
        compiler_params=_params("parallel", "arbitrary"),
        name="inproj",
    )(x, ms.arr, ms.arr, w_bf16)


def _hgrn_consts(c, lseq):
    nlev = int(math.log2(lseq))
    r = np.arange(c)
    seg, pos = r // lseq, r % lseq
    j = np.arange(c)
    same = seg[:, None] == seg[None, :]
    mats = [same & (j[None, :] <= r[:, None]),
            same & (j[None, :] > r[:, None])]
    rq = np.zeros((nlev, c, 1), np.float32)
    mb = np.zeros((nlev + 1, c, c), np.float32)
    for lev in range(nlev):
        half, blk = 1 << lev, 2 << lev
        is_q = (pos % blk) >= half
        ref = (r // blk) * blk + half - 1
        m = np.where(is_q[:, None],
                     (j[None, :] > ref[:, None]) & (j[None, :] <= r[:, None]),
                     (j[None, :] > r[:, None]) & (j[None, :] <= ref[:, None]))
        mats.append(m)
        rq[lev, :, 0] = is_q
        mb[lev] = (r[:, None] // blk) == (r[None, :] // blk)
    mb[nlev] = np.eye(c)
    a = np.concatenate(mats, axis=0).astype(np.float32)
    rq = np.broadcast_to(rq, (nlev, c, LANES)).copy()
    mb2 = np.concatenate([mb, mb], axis=1)
    a2 = np.concatenate([a, a], axis=1)
    return jnp.asarray(a2, BF16), jnp.asarray(rq), jnp.asarray(mb2), nlev


def _hgrn_kernel(*refs, c, lseq, nch, nlev, has_s0, npp):
    if has_s0:
        q_ref, f_ref, i_ref, lb_ref, a_ref, rq_ref, mb_ref, s0_ref, o_ref, s_ref, st_ref = refs
    else:
        q_ref, f_ref, i_ref, lb_ref, a_ref, rq_ref, mb_ref, o_ref, s_ref, st_ref = refs
    nsq = c // lseq
    lane_lo = lax.broadcasted_iota(jnp.int32, (c, LANES), 1) < HEAD_DIM
    r128 = lax.broadcasted_iota(jnp.int32, (LANES, nsq * LANES), 0)
    c128 = lax.broadcasted_iota(jnp.int32, (LANES, nsq * LANES), 1) % LANES
    same_head = (r128 < HEAD_DIM) == (c128 < HEAD_DIM)
    row_seq = lax.broadcasted_iota(jnp.int32, (c, LANES), 0) // lseq
    zero64 = jnp.zeros((HEAD_DIM, HEAD_DIM), F32)

    if has_s0:
        for pp in range(npp):
            for s in range(nsq):
                top = jnp.concatenate([s0_ref[s, 2 * pp], zero64], axis=1)
                bot = jnp.concatenate([zero64, s0_ref[s, 2 * pp + 1]], axis=1)
                st_ref[pp, :, s * LANES:(s + 1) * LANES] = jnp.concatenate([top, bot], axis=0).T
    else:
        st_ref[...] = jnp.zeros_like(st_ref)

    def expand(x):
        if nsq == 1:
            return x
        return jnp.concatenate([jnp.where(row_seq == s, x, 0.0) for s in range(nsq)], axis=1)

    def heads_on_rows(x):
        return jnp.concatenate([jnp.where(lane_lo, x, 0.0), jnp.where(lane_lo, 0.0, x)], axis=0)

    def chunk(ci, carry):
        r0 = pl.multiple_of(ci * c, c)
        for pp in range(npp):
            ls = slice(pp * LANES, (pp + 1) * LANES)
            lbv = lb_ref[:, ls]
            qa = q_ref[pl.ds(r0, c), ls]
            z = f_ref[pl.ds(r0, c), ls]
            v = i_ref[pl.ds(r0, c), ls]
            lf = jnp.log(lbv + (1.0 - lbv) * jax.nn.sigmoid(z))
            kin = (1.0 - lbv) * jax.nn.sigmoid(-z)
            qs = qa * jax.nn.sigmoid(qa)
            hi = lf.astype(BF16)
            lo = (lf - hi.astype(F32)).astype(BF16)
            g = _dot(a_ref[...], jnp.concatenate([hi, lo], axis=0))
            eb = jnp.exp(g[0:c])
            qg = qs * eb
            kd = kin * jnp.exp(g[c:2 * c])
            st = st_ref[pp]
            o = _dot_nt(expand(qg).astype(BF16), st.astype(BF16))

            kin_b = kin.astype(BF16)
            sc = mb_ref[nlev] * _dot_nt(heads_on_rows(qs).astype(BF16), kin_b)
            for lev in range(nlev):
                y = jnp.exp(g[(2 + lev) * c:(3 + lev) * c])
                isq = rq_ref[lev]
                ql = qs * (y * isq)
                kl = kin * (y * (1.0 - isq))
                sc = sc + mb_ref[lev] * _dot_nt(heads_on_rows(ql).astype(BF16), kl.astype(BF16))
            r = _dot(sc.astype(BF16), v.astype(BF16))
            o_ref[pl.ds(r0, c), ls] = o + jnp.where(lane_lo, r[0:c], r[c:2 * c])

            if nsq == 1:
                dl = eb[c - 1:c, :]
            else:
                dl = jnp.concatenate([eb[(s + 1) * lseq - 1:(s + 1) * lseq, :] for s in range(nsq)], axis=1)
            u = _dot(v.T.astype(BF16), expand(kd).astype(BF16))
            st_ref[pp] = st * dl + jnp.where(same_head, u, 0.0)
        return carry

    lax.fori_loop(0, nch, chunk, 0)

    for pp in range(npp):
        for s in range(nsq):
            sbd = st_ref[pp, :, s * LANES:(s + 1) * LANES].T
            s_ref[s, 2 * pp] = sbd[0:HEAD_DIM, 0:HEAD_DIM]
            s_ref[s, 2 * pp + 1] = sbd[HEAD_DIM:, HEAD_DIM:]


def _hgrn(proj, lb, s0, nseq, t):
    _, n, width = proj.shape
    npair = width // LANES
    c = LANES
    lseq = min(t, c)
    nsq = c // lseq
    if t >= c:
        nch, rows, ngroups = t // c, t, nseq
    else:
        nch, rows, ngroups = 1, c, nseq // nsq
    npp = 4 if (nsq == 1 and npair % 4 == 0) else 1
    wl = npp * LANES
    a, rq, mb, nlev = _hgrn_consts(c, lseq)
    has_s0 = s0 is not None
    ins = [proj, proj, proj, lb.reshape(1, width), a, rq, mb]
    in_specs = [pl.BlockSpec((None, rows, wl), lambda g, p: (0, g, p)),
                pl.BlockSpec((None, rows, wl), lambda g, p: (1, g, p)),
                pl.BlockSpec((None, rows, wl), lambda g, p: (2, g, p)),
                pl.BlockSpec((1, wl), lambda g, p: (0, p)),
                pl.BlockSpec(a.shape, lambda g, p: (0, 0)),
                pl.BlockSpec(rq.shape, lambda g, p: (0, 0, 0)),
                pl.BlockSpec(mb.shape, lambda g, p: (0, 0, 0))]
    if has_s0:
        ins.append(s0)
        in_specs.append(pl.BlockSpec((nsq, 2 * npp, HEAD_DIM, HEAD_DIM), lambda g, p: (g, p, 0, 0)))
    kern = functools.partial(_hgrn_kernel, c=c, lseq=lseq, nch=nch, nlev=nlev, has_s0=has_s0, npp=npp)
    return pl.pallas_call(
        kern,
        out_shape=(jax.ShapeDtypeStruct((n, width), F32),
                   jax.ShapeDtypeStruct((nseq, 2 * npair, HEAD_DIM, HEAD_DIM), F32)),
        grid=(ngroups, npair // npp),
        in_specs=in_specs,
        out_specs=(pl.BlockSpec((rows, wl), lambda g, p: (g, p)),
                   pl.BlockSpec((nsq, 2 * npp, HEAD_DIM, HEAD_DIM), lambda g, p: (g, p, 0, 0))),
        scratch_shapes=[pltpu.VMEM((npp, LANES, nsq * LANES), F32)],
        compiler_params=_params("parallel", "parallel"),
        name="hgrn",
    )(*ins)


LOG2E = 1.4426950408889634


def _sb_block(z2, tail, uneg, mask):
    sp = jnp.maximum(z2, 0.0) + jnp.log2(1.0 + jnp.exp2(-jnp.abs(z2)))
    if mask is not None:
        sp = jnp.where(mask, sp, 0.0)
    rr = _dot(sp.astype(BF16), uneg)
    a = jnp.exp2(z2 - sp + (tail + rr))
    if mask is not None:
        a = jnp.where(mask, a, 0.0)
    return a, tail - jnp.sum(sp, axis=1, keepdims=True)


def _uneg_const(tk):
    j = np.arange(tk)
    return jnp.asarray(-(j[:, None] > j[None, :]).astype(np.float32), BF16)


def _sb_kernel(q_ref, k_ref, v_ref, bias_ref, u2_ref, o_ref, *, tq, tk, npp):
    qi = pl.program_id(2)
    u2 = u2_ref[...]
    scale = HEAD_DIM ** -0.5 * LOG2E
    lane_lo = lax.broadcasted_iota(jnp.int32, (tq, LANES), 1) < HEAD_DIM
    ndiag = tq // tk
    base = qi * ndiag
    below_diag = (lax.broadcasted_iota(jnp.int32, (tk, tk), 1) < lax.broadcasted_iota(jnp.int32, (tk, tk), 0))
    lanes = [slice(pp * LANES, (pp + 1) * LANES) for pp in range(npp)]
    qh, bias = [], []
    for pp in range(npp):
        q = q_ref[:, lanes[pp]] * scale
        qh.append((jnp.where(lane_lo, q, 0.0).astype(BF16), jnp.where(lane_lo, 0.0, q).astype(BF16)))
        h0 = 2 * (pl.program_id(1) * npp + pp)
        bias.append((bias_ref[h0] * LOG2E, bias_ref[h0 + 1] * LOG2E))

    def logits(qpair, kb, pp):
        kk = k_ref[pl.ds(pl.multiple_of(kb * tk, tk), tk), lanes[pp]].astype(BF16)
        return tuple(_dot_nt(qpair[h], kk) + bias[pp][h] for h in range(2))

    def finish(z, kb, carry, mask, pp):
        vv = v_ref[pl.ds(pl.multiple_of(kb * tk, tk), tk), lanes[pp]].astype(BF16)
        out = []
        for h in range(2):
            acc, tail = carry[h]
            a, tail = _sb_block(z[h], tail, u2, mask)
            out.append((acc + _dot(a.astype(BF16), vv), tail))
        return tuple(out)

    carries = []
    for pp in range(npp):
        bands = []
        for r in range(ndiag):
            qband = tuple(x[r * tk:(r + 1) * tk] for x in qh[pp])
            zero = (jnp.zeros((tk, LANES), F32), jnp.zeros((tk, 1), F32))
            c = finish(logits(qband, base + r, pp), base + r, (zero, zero), below_diag, pp)
            for kb in range(r - 1, -1, -1):
                c = finish(logits(qband, base + kb, pp), base + kb, c, None, pp)
            bands.append(c)
        carries.append(tuple(tuple(jnp.concatenate([b[h][j] for b in bands], axis=0) for j in range(2))
                             for h in range(2)))

    def body(i, state):
        cs, zs = state
        kb = base - 1 - i
        z_next = tuple(logits(qh[pp], jnp.maximum(kb - 1, 0), pp) for pp in range(npp))
        return tuple(finish(zs[pp], kb, cs[pp], None, pp) for pp in range(npp)), z_next

    z0 = tuple(logits(qh[pp], jnp.maximum(base - 1, 0), pp) for pp in range(npp))
    carries, _ = lax.fori_loop(0, base, body, (tuple(carries), z0))
    for pp in range(npp):
        o_ref[:, lanes[pp]] = jnp.where(lane_lo, carries[pp][0][0], carries[pp][1][0])


def _sb_prompt(proj, bias, nseq, t, tq, tk):
    _, n, width = proj.shape
    npair = width // LANES
    npp = 2 if npair % 2 == 0 else 1
    wl = npp * LANES
    nq = t // tq
    u2 = _uneg_const(tk)
    kern = functools.partial(_sb_kernel, tq=tq, tk=tk, npp=npp)
    return pl.pallas_call(
        kern,
        out_shape=jax.ShapeDtypeStruct((n, width), F32),
        grid=(nseq, npair // npp, nq),
        in_specs=[pl.BlockSpec((None, tq, wl), lambda b, p, i: (4, b * nq + i, p)),
                  pl.BlockSpec((None, t, wl), lambda b, p, i: (5, b, p)),
                  pl.BlockSpec((None, t, wl), lambda b, p, i: (6, b, p)),
                  pl.BlockSpec(memory_space=pltpu.SMEM),
                  pl.BlockSpec(u2.shape, lambda b, p, i: (0, 0))],
        out_specs=pl.BlockSpec((tq, wl), lambda b, p, i: (b * nq + i, p)),
        compiler_params=_params("parallel", "parallel", "arbitrary"),
        name="sb",
    )(proj, proj, proj, bias, u2)


def _sbpaged_kernel(pt_ref, q_ref, kn_ref, vn_ref, bias_ref, u2_ref, *rest, npages, page, t, nh):
    k_refs = rest[:npages]
    v_refs = rest[npages:2 * npages]
    o_ref = rest[2 * npages]
    rows = nh * t
    scale = HEAD_DIM ** -0.5 * LOG2E
    u2 = u2_ref[...]
    qa = (q_ref[...] * scale).astype(BF16)
    bias = bias_ref[...]
    head_z = lax.broadcasted_iota(jnp.int32, (rows, page), 0) // t
    head_o = lax.broadcasted_iota(jnp.int32, (rows, HEAD_DIM), 0) // t

    order = list(range(npages - 1, -1, -1))
    kblk = [kn_ref] + [k_refs[p] for p in order]
    vblk = [vn_ref] + [v_refs[p] for p in order]
    nblk = npages + 1

    def logits(x_ref):
        z = bias
        for h in range(nh):
            z = z + jnp.where(head_z == h, _dot(qa, x_ref[h].astype(BF16)), 0.0)
        return z

    z2 = jnp.concatenate([logits(x) for x in kblk], axis=0)
    qpos = lax.broadcasted_iota(jnp.int32, (rows, page), 0) % t
    kpos = lax.broadcasted_iota(jnp.int32, (rows, page), 1)
    visible = jnp.concatenate([kpos < qpos] + [jnp.ones((rows, page), jnp.bool_)] * npages, axis=0)
    sp = jnp.where(visible, jnp.maximum(z2, 0.0) + jnp.log2(1.0 + jnp.exp2(-jnp.abs(z2))), 0.0)
    rr = _dot(sp.astype(BF16), u2)
    blk_sum = jnp.sum(sp, axis=1, keepdims=True)
    tails = [jnp.zeros((rows, 1), F32)]
    for b in range(nblk - 1):
        tails.append(tails[-1] - blk_sum[b * rows:(b + 1) * rows])
    a = jnp.where(visible, jnp.exp2(z2 - sp + (jnp.concatenate(tails, axis=0) + rr)), 0.0).astype(BF16)
    acc = jnp.zeros((rows, HEAD_DIM), F32)
    for b in range(nblk):
        a_b = a[b * rows:(b + 1) * rows]
        for h in range(nh):
            acc = acc + jnp.where(head_o == h, _dot_nt(a_b, vblk[b][h].astype(BF16)), 0.0)
    o_ref[...] = acc


def _sb_paged(q, kn_t, vn_t, cache_kt, cache_vt, layer, page_table, sb_bias):
    nseq, rows, _ = q.shape
    nh, page = cache_kt.shape[2], cache_kt.shape[4]
    t = rows // nh
    npages = page_table.shape[1]
    u2 = _uneg_const(page)
    bias = jnp.broadcast_to(jnp.repeat(sb_bias.astype(F32) * LOG2E, t)[:, None], (rows, page))
    kern = functools.partial(_sbpaged_kernel, npages=npages, page=page, t=t, nh=nh)

    def page_spec(p):
        return pl.BlockSpec((None, None, nh, HEAD_DIM, page), lambda b, pt: (layer, pt[b, p], 0, 0, 0))

    new_spec = pl.BlockSpec((None, nh, HEAD_DIM, page), lambda b, pt: (b, 0, 0, 0))
    grid_spec = pltpu.PrefetchScalarGridSpec(
        num_scalar_prefetch=1,
        grid=(nseq,),
        in_specs=[pl.BlockSpec((None, rows, HEAD_DIM), lambda b, pt: (b, 0, 0)),
                  new_spec, new_spec,
                  pl.BlockSpec((rows, page), lambda b, pt: (0, 0)),
                  pl.BlockSpec(u2.shape, lambda b, pt: (0, 0))]
                 + [page_spec(p) for p in range(npages)] * 2,
        out_specs=pl.BlockSpec((None, rows, HEAD_DIM), lambda b, pt: (b, 0, 0)),
    )
    return pl.pallas_call(
        kern,
        out_shape=jax.ShapeDtypeStruct((nseq, rows, HEAD_DIM), F32),
        grid_spec=grid_spec,
        compiler_params=_params("arbitrary"),
        name="sbpaged",
    )(page_table, q, kn_t, vn_t, bias, u2, *([cache_kt] * npages), *([cache_vt] * npages))


def _layernorm_rows(x, g, b):
    mu = jnp.mean(x, axis=-1, keepdims=True)
    xc = x - mu
    var = jnp.mean(xc * xc, axis=-1, keepdims=True)
    return xc * lax.rsqrt(var + LN_EPS) * g + b


def _head_rms(o, hsum):
    sq = o * o
    hi = sq.astype(BF16)
    lo = (sq - hi.astype(F32)).astype(BF16)
    ms = (_dot(hi, hsum) + _dot(lo, hsum)) * (1.0 / HEAD_DIM)
    return o * lax.rsqrt(ms + RMS_EPS)


def _outproj_kernel(oa_ref, ga_ref, ob_ref, x_ref, g1_ref, sh2_ref, sc2_ref, na_ref, nb_ref, hs_ref,
                    w_ref, lg_ref, lb_ref, x1_ref, h2t_ref, *, alpha):
    hs = hs_ref[...]
    ga = ga_ref[...]
    ya = _head_rms(oa_ref[...], hs) * na_ref[...] * (ga * jax.nn.sigmoid(ga))
    yb = _head_rms(ob_ref[...], hs) * nb_ref[...]
    wa = w_ref.shape[0] // 2
    mix = _dot(ya.astype(BF16), w_ref[0:wa, :]) + _dot(yb.astype(BF16), w_ref[wa:, :])
    x1 = _layernorm_rows(alpha * x_ref[...] + (1.0 + g1_ref[...]) * mix, lg_ref[...], lb_ref[...])
    x1_ref[...] = x1
    h2 = x1 * (1.0 + sc2_ref[...]) + sh2_ref[...]
    h2t_ref[...] = h2.T.astype(BF16)


def _outproj(oa, proj, ob, x, ms, norm_a, norm_b, w_out_bf16, ln_g, ln_b, alpha):
    n, d = x.shape
    tm = ms.tm
    wa = oa.shape[1]
    hs = (np.arange(wa)[:, None] // HEAD_DIM == np.arange(wa)[None, :] // HEAD_DIM).astype(np.float32)
    row = lambda i: (i, 0)
    fixed = lambda i: (0, 0)
    return pl.pallas_call(
        functools.partial(_outproj_kernel, alpha=alpha),
        out_shape=(jax.ShapeDtypeStruct((n, d), F32), jax.ShapeDtypeStruct((d, n), BF16)),
        grid=(n // tm,),
        in_specs=[pl.BlockSpec((tm, wa), row),
                  pl.BlockSpec((None, tm, wa), lambda i: (3, i, 0)),
                  pl.BlockSpec((tm, wa), row),
                  pl.BlockSpec((tm, d), row),
                  ms.spec(2, 1), ms.spec(3, 1), ms.spec(4, 1),
                  pl.BlockSpec((1, wa), fixed), pl.BlockSpec((1, wa), fixed),
                  pl.BlockSpec((wa, wa), fixed),
                  pl.BlockSpec((2 * wa, d), fixed),
                  pl.BlockSpec((1, d), fixed), pl.BlockSpec((1, d), fixed)],
        out_specs=(pl.BlockSpec((tm, d), row), pl.BlockSpec((d, tm), lambda i: (0, i))),
        compiler_params=_params("parallel"),
        name="outproj",
    )(oa, proj, ob, x, ms.arr, ms.arr, ms.arr, norm_a.reshape(1, wa), norm_b.reshape(1, wa),
      jnp.asarray(hs, BF16), w_out_bf16, ln_g.reshape(1, d), ln_b.reshape(1, d))


MARK0 = -(2.0 ** 127)
MSTEP = 2.0 ** 105
CAND_ROWS = 64


def _stair_cells():
    k = PEER_TOPK
    return [(a, b) for a in range(k) for b in range(k) if (a + 1) * (b + 1) <= k]


def _cand_consts():
    cells = _stair_cells()
    g1 = np.zeros((CAND_ROWS, PEER_TOPK), np.float32)
    g2 = np.zeros((CAND_ROWS, PEER_TOPK), np.float32)
    for r, (a, b) in enumerate(cells):
        g1[r, a] = 1.0
        g2[r, b] = 1.0
    return jnp.asarray(g1, BF16), jnp.asarray(g2, BF16), jnp.asarray(g1.T.copy(), BF16), len(cells)


def _extract_topk(s, k):
    rows, tn = s.shape
    ridx = lax.broadcasted_iota(jnp.int32, (rows, tn), 0)
    kidx = lax.broadcasted_iota(jnp.int32, (k, tn), 0)

    def body(i, carry):
        cur, vals, rank = carry
        m = jnp.max(cur, axis=0, keepdims=True)
        first = jnp.min(jnp.where(cur == m, ridx, rows), axis=0, keepdims=True)
        hit = ridx == first
        cur = jnp.where(hit, -jnp.inf, cur)
        rank = jnp.where(hit, jnp.full((1, tn), i, jnp.int32).astype(F32), rank)
        vals = jnp.where(kidx == i, m, vals)
        return cur, vals, rank

    init = (s, jnp.zeros((k, tn), F32), jnp.full((rows, tn), float(k), F32))
    _, vals, rank = lax.fori_loop(0, k, body, init)
    return vals, rank


def _knock_topk(cur_ref, val_ref, narr, k):
    tn = cur_ref.shape[2]

    def body(i, carry):
        mark = MARK0 - jnp.full((1, tn), i, jnp.int32).astype(F32) * MSTEP
        for a in range(narr):
            cur = cur_ref[a]
            m = jnp.max(cur, axis=0, keepdims=True)
            cur_ref[a] = jnp.where(cur == m, mark, cur)
            val_ref[a, pl.ds(i, 1), :] = m
        return carry

    lax.fori_loop(0, k, body, 0)


def _tie_free(cur_ref, narr, k, valid=None):
    ok = None
    for a in range(narr):
        gone = cur_ref[a] <= MARK0
        if valid is not None:
            gone = gone & valid
        good = jnp.sum(gone.astype(F32), axis=0, keepdims=True) == float(k)
        ok = good if ok is None else (ok & good)
    return jnp.min(ok.astype(jnp.int32)) == 1


def _exact_topk_into(src_ref, cur_ref, val_ref, narr, k):
    for a in range(narr):
        s = src_ref[a]
        vals, rank = _extract_topk(s, k)
        val_ref[a] = vals
        cur_ref[a] = jnp.where(rank < float(k), MARK0 - rank * MSTEP, s)


def _bcast_rows_bf16(row, nrows):
    tile = jnp.broadcast_to(row, (16, row.shape[1])).astype(BF16)
    return jnp.concatenate([tile] * (nrows // 16), axis=0)


def _peertopk_kernel(ht_ref, wqt_ref, k1_ref, k2_ref, g1_ref, g2_ref, ga_ref, c1_ref, a1_ref, r2_ref, a2_ref,
                     s_scr, cur_scr, val_scr, cand_scr, ccur_scr, cval_scr, *, nheads, nkeys, ncells):
    tn = ht_ref.shape[1]
    dq = k1_ref.shape[1]
    k = PEER_TOPK
    qt = _dot(wqt_ref[...], ht_ref[...]).astype(BF16)
    k1 = k1_ref[...].astype(BF16)
    k2 = k2_ref[...].astype(BF16)
    for h in range(nheads):
        s_scr[2 * h] = _dot(k1, qt[(2 * h) * dq:(2 * h + 1) * dq, :])
        s_scr[2 * h + 1] = _dot(k2, qt[(2 * h + 1) * dq:(2 * h + 2) * dq, :])
    cur_scr[...] = s_scr[...]
    _knock_topk(cur_scr, val_scr, 2 * nheads, k)

    @pl.when(jnp.logical_not(_tie_free(cur_scr, 2 * nheads, k)))
    def _():
        _exact_topk_into(s_scr, cur_scr, val_scr, 2 * nheads, k)

    g1, g2 = g1_ref[...], g2_ref[...]
    valid = lax.broadcasted_iota(jnp.int32, (CAND_ROWS, tn), 0) < ncells

    def pick(g, v):
        hi, mid, lo = _split3(v)
        return _dot(g, hi) + _dot(g, mid) + _dot(g, lo)

    for h in range(nheads):
        cand = jnp.where(valid, pick(g1, val_scr[2 * h]) + pick(g2, val_scr[2 * h + 1]), -jnp.inf)
        cand_scr[h] = cand
        ccur_scr[h] = cand
    _knock_topk(ccur_scr, cval_scr, nheads, k)

    @pl.when(jnp.logical_not(_tie_free(ccur_scr, nheads, k, valid)))
    def _():
        _exact_topk_into(cand_scr, ccur_scr, cval_scr, nheads, k)

    ga = ga_ref[...]
    inv_step = 1.0 / MSTEP
    for h in range(nheads):
        taken = ((ccur_scr[h] <= MARK0) & valid).astype(BF16)
        cnt = _dot(ga, taken)
        top = cval_scr[h]
        inv_z = 1.0 / jnp.sum(jnp.exp(top - top[0:1]), axis=0, keepdims=True)
        s1, s2 = s_scr[2 * h], s_scr[2 * h + 1]
        cur1, cur2 = cur_scr[2 * h], cur_scr[2 * h + 1]
        sel1, sel2 = cur1 <= MARK0, cur2 <= MARK0
        rank1 = jnp.where(sel1, (MARK0 - cur1) * inv_step, float(k)).astype(BF16)
        rank2 = jnp.where(sel2, (MARK0 - cur2) * inv_step, float(k))
        c1 = jnp.zeros((nkeys, tn), BF16)
        for a in range(k):
            c1 = jnp.where(rank1 == a, _bcast_rows_bf16(cnt[a:a + 1], nkeys), c1)
        v1 = val_scr[2 * h]
        v2 = val_scr[2 * h + 1]
        c1_ref[h] = c1.astype(F32)
        a1_ref[h] = jnp.where(sel1, jnp.exp(s1 - v1[0:1]) * inv_z, 0.0)
        r2_ref[h] = rank2.astype(BF16)
        a2_ref[h] = jnp.where(sel2, jnp.exp(s2 - v2[0:1]), 0.0).astype(BF16)


def _peer_topk(h2t, wqt_bf16, k1, k2, tn):
    d, n = h2t.shape
    nkeys, dq = k1.shape
    nheads = wqt_bf16.shape[0] // (2 * dq)
    g1, g2, ga, ncells = _cand_consts()
    shp = (nheads, nkeys, n)
    blk = pl.BlockSpec((nheads, nkeys, tn), lambda i: (0, 0, i))
    fixed = lambda i: (0, 0)
    k = PEER_TOPK
    return pl.pallas_call(
        functools.partial(_peertopk_kernel, nheads=nheads, nkeys=nkeys, ncells=ncells),
        out_shape=(jax.ShapeDtypeStruct(shp, F32), jax.ShapeDtypeStruct(shp, F32),
                   jax.ShapeDtypeStruct(shp, BF16), jax.ShapeDtypeStruct(shp, BF16)),
        grid=(n // tn,),
        in_specs=[pl.BlockSpec((d, tn), lambda i: (0, i)),
                  pl.BlockSpec(wqt_bf16.shape, fixed),
                  pl.BlockSpec((nkeys, dq), fixed), pl.BlockSpec((nkeys, dq), fixed),
                  pl.BlockSpec(g1.shape, fixed), pl.BlockSpec(g2.shape, fixed), pl.BlockSpec(ga.shape, fixed)],
        out_specs=(blk, blk, blk, blk),
        scratch_shapes=[pltpu.VMEM((2 * nheads, nkeys, tn), F32), pltpu.VMEM((2 * nheads, nkeys, tn), F32),
                        pltpu.VMEM((2 * nheads, k, tn), F32),
                        pltpu.VMEM((nheads, CAND_ROWS, tn), F32), pltpu.VMEM((nheads, CAND_ROWS, tn), F32),
                        pltpu.VMEM((nheads, k, tn), F32)],
        compiler_params=_params("parallel"),
        name="peertopk",
    )(h2t, wqt_bf16, k1, k2, g1, g2, ga)


GATE_LANES = 256


def _gelu_tanh(x):
    c = 0.7978845608028654
    hx = 0.5 * x
    return hx + hx * jnp.tanh(x * (c + (c * 0.044715) * (x * x)))


def _peermain_kernel(ht_ref, u_ref, vt_ref, c1_ref, a1_ref, r2_ref, a2_ref, o_ref, act_ref, g_ref, *, nkeys, nheads):
    j = pl.program_id(1)
    te, tn = act_ref.shape
    gl = min(GATE_LANES, tn)
    assert tn % gl == 0
    zero = jnp.zeros((), BF16)

    @pl.when(j == 0)
    def _():
        o_ref[...] = jnp.zeros_like(o_ref)

    act_ref[...] = _dot(u_ref[...], ht_ref[...])
    for sub in range(te // nkeys):
        arow = slice(sub * nkeys, (sub + 1) * nkeys)
        for lt in range(tn // gl):
            cols = slice(lt * gl, (lt + 1) * gl)
            w = jnp.zeros((nkeys, gl), BF16)
            for h in range(nheads):
                c1 = _bcast_rows_bf16(c1_ref[h, sub:sub + 1, cols], nkeys)
                a1 = _bcast_rows_bf16(a1_ref[h, sub:sub + 1, cols], nkeys)
                w = w + a1 * jnp.where(r2_ref[h, :, cols] < c1, a2_ref[h, :, cols], zero)
            g_ref[arow, cols] = w * _gelu_tanh(act_ref[arow, cols]).astype(BF16)
    o_ref[...] += _dot(vt_ref[...], g_ref[...])


def _peer_main(h2t, u_bf16, vt_bf16, c1, a1, r2, a2, tn, te):
    d, n = h2t.shape
    ne = u_bf16.shape[0]
    nheads, nkeys, _ = c1.shape
    small = pl.BlockSpec((nheads, nkeys, tn), lambda i, j: (0, 0, i))
    per_tile = pl.BlockSpec((nheads, te // nkeys, tn), lambda i, j: (0, j, i))
    return pl.pallas_call(
        functools.partial(_peermain_kernel, nkeys=nkeys, nheads=nheads),
        out_shape=jax.ShapeDtypeStruct((d, n), F32),
        grid=(n // tn, ne // te),
        in_specs=[pl.BlockSpec((d, tn), lambda i, j: (0, i)),
                  pl.BlockSpec((te, d), lambda i, j: (j, 0)),
                  pl.BlockSpec((d, te), lambda i, j: (0, j)),
                  per_tile, per_tile, small, small],
        out_specs=pl.BlockSpec((d, tn), lambda i, j: (0, i)),
        scratch_shapes=[pltpu.VMEM((te, tn), F32), pltpu.VMEM((te, tn), BF16)],
        compiler_params=_params("parallel", "arbitrary"),
        name="peermain",
    )(h2t, u_bf16, vt_bf16, c1, a1, r2, a2)


def _ln2_kernel(x1_ref, fft_ref, g2_ref, lg_ref, lb_ref, y_ref, *, alpha):
    ff = fft_ref[...].T
    y_ref[...] = _layernorm_rows(alpha * x1_ref[...] + (1.0 + g2_ref[...]) * ff, lg_ref[...], lb_ref[...])


def _ln2(x1, fft, col0, ms, ln_g, ln_b, alpha):
    n, d = x1.shape
    tm = ms.tm
    c0 = col0 // tm
    return pl.pallas_call(
        functools.partial(_ln2_kernel, alpha=alpha),
        out_shape=jax.ShapeDtypeStruct((n, d), F32),
        grid=(n // tm,),
        in_specs=[pl.BlockSpec((tm, d), lambda i: (i, 0)),
                  pl.BlockSpec((d, tm), lambda i: (0, c0 + i)),
                  ms.spec(5, 1),
                  pl.BlockSpec((1, d), lambda i: (0, 0)), pl.BlockSpec((1, d), lambda i: (0, 0))],
        out_specs=pl.BlockSpec((tm, d), lambda i: (i, 0)),
        compiler_params=_params("parallel"),
        name="ln2",
    )(x1, fft, ms.arr, ln_g.reshape(1, d), ln_b.reshape(1, d))


def _pick(n, pref):
    for t in pref:
        if n % t == 0:
            return t
    raise ValueError(f"no tile for {n}")


def kernel(x_prompt, x_sample, cache_k, cache_v, state_hgrn, page_table, c_prompt, c_sample, w_ada, b_ada, w_in, lb_raw, norm_a, norm_b, sb_bias, w_out, ln1_g, ln1_b, ln2_g, ln2_b, peer_wq, peer_k1, peer_k2, peer_u, peer_v):
    depth = w_ada.shape[0]
    bp, tp, d = x_prompt.shape
    bs, ts, _ = x_sample.shape
    n_p, n_s = bp * tp, bs * ts
    page, hb = cache_k.shape[2], cache_k.shape[3]
    wb = hb * HEAD_DIM
    alpha = (2.0 * depth) ** 0.25
    lb_all = jnp.cumsum(jax.nn.softmax(lb_raw.astype(F32), axis=0), axis=0)

    tm_p = _pick(tp, (1024, 512, 256, 128))
    tm_s = _pick(n_s, (1024, 512, 256, 128))
    assert n_p % tm_s == 0
    n_all = n_p + n_s
    tn_k = _pick(n_all, (512, 256, 128))
    tn_m = _pick(n_all, (512, 256, 128))
    tq = _pick(tp, (512, 256, 128))
    tk = min(tq, 256)

    cache_kt = jnp.transpose(cache_k, (0, 1, 3, 4, 2))
    cache_vt = jnp.transpose(cache_v, (0, 1, 3, 4, 2))
    yp = x_prompt.reshape(n_p, d)
    ys = x_sample.reshape(n_s, d)
    outs = {k: [] for k in ("kp", "vp", "sp", "ks", "vs", "ss")}
    for l in range(depth):
        mod = _mod(jnp.concatenate([c_prompt, c_sample], axis=0), w_ada[l], b_ada[l])
        ms_p = _ModSpec(mod[:bp].reshape(bp, 1, 6 * d), False, tm_p, tp // tm_p, d)
        ms_s = _ModSpec(jnp.repeat(mod[bp:], ts, axis=0), True, tm_s, 1, d)
        w_in_b = w_in[l].astype(BF16)
        w_out_b = w_out[l].astype(BF16)
        lb = lb_all[l]

        proj_p = _inproj(yp, ms_p, w_in_b, wb)
        proj_s = _inproj(ys, ms_s, w_in_b, wb)

        oa_p, sp = _hgrn(proj_p, lb, None, bp, tp)
        oa_s, ss = _hgrn(proj_s, lb, state_hgrn[l], bs, ts)

        ob_p = _sb_prompt(proj_p, sb_bias[l].astype(F32), bp, tp, tq, tk)
        q_hi = proj_s[4].reshape(bs, ts, hb, HEAD_DIM).transpose(0, 2, 1, 3).reshape(bs, hb * ts, HEAD_DIM)
        new_t = lambda a: jnp.pad(a.reshape(bs, ts, hb, HEAD_DIM).transpose(0, 2, 3, 1),
                                  ((0, 0), (0, 0), (0, 0), (0, page - ts)))
        ob_s = _sb_paged(q_hi, new_t(proj_s[5]), new_t(proj_s[6]), cache_kt, cache_vt, l, page_table, sb_bias[l])
        ob_s = ob_s.reshape(bs, hb, ts, HEAD_DIM).transpose(0, 2, 1, 3).reshape(n_s, wb)

        x1_p, h2t_p = _outproj(oa_p, proj_p, ob_p, yp, ms_p, norm_a[l], norm_b[l], w_out_b, ln1_g[l], ln1_b[l], alpha)
        x1_s, h2t_s = _outproj(oa_s, proj_s, ob_s, ys, ms_s, norm_a[l], norm_b[l], w_out_b, ln1_g[l], ln1_b[l], alpha)

        h2t = jnp.concatenate([h2t_p, h2t_s], axis=1)
        c1, a1, r2, a2 = _peer_topk(h2t, peer_wq[l].T.astype(BF16), peer_k1[l], peer_k2[l], tn_k)
        fft = _peer_main(h2t, peer_u[l].astype(BF16), peer_v[l].T.astype(BF16), c1, a1, r2, a2,
                         tn_m, 16 * peer_k1.shape[1])

        yp = _ln2(x1_p, fft, 0, ms_p, ln2_g[l], ln2_b[l], alpha)
        ys = _ln2(x1_s, fft, n_p, ms_s, ln2_g[l], ln2_b[l], alpha)

        outs["kp"].append(proj_p[5].reshape(bp, tp, hb, HEAD_DIM))
        outs["vp"].append(proj_p[6].reshape(bp, tp, hb, HEAD_DIM))
        outs["sp"].append(sp)
        outs["ks"].append(proj_s[5].reshape(bs, ts, hb, HEAD_DIM))
        outs["vs"].append(proj_s[6].reshape(bs, ts, hb, HEAD_DIM))
        outs["ss"].append(ss)
    return (yp.reshape(bp, tp, d), ys.reshape(bs, ts, d),
            jnp.stack(outs["kp"]), jnp.stack(outs["vp"]), jnp.stack(outs["sp"]),
            jnp.stack(outs["ks"]), jnp.stack(outs["vs"]), jnp.stack(outs["ss"]))
```
